```python
import jax, jax.numpy as jnp
from jax import lax
import numpy as np

D_MODEL = 1024
BATCH = 4
SEQ = 8192
DEPTH = 1
DEC_BATCH = 4
DEC_SEQ = 4096
PAST_LEN = 128

GRID_W = 64
N_HEADS = 8
N_KV_HEADS = 2
HEAD_DIM = 64
ATTN_WIDTH = N_HEADS * HEAD_DIM
KV_WIDTH = N_KV_HEADS * HEAD_DIM
POOL_WINDOWS = (2, 4, 8, 16)
POOL_GROUPS = len(POOL_WINDOWS)
POOL_WIDTH = D_MODEL // 2
POOL_GROUP_WIDTH = POOL_WIDTH // POOL_GROUPS
D_FF = 2816
PLE_DIM = 256
Q_BLOCK = 128
ROPE_THETA = 10000.0
EPS = 1e-6
SPLITS = (ATTN_WIDTH,
          ATTN_WIDTH + KV_WIDTH,
          ATTN_WIDTH + 2 * KV_WIDTH,
          ATTN_WIDTH + 2 * KV_WIDTH + POOL_WIDTH,
          ATTN_WIDTH + 2 * KV_WIDTH + POOL_WIDTH + D_MODEL)
IN_WIDTH = ATTN_WIDTH + 2 * KV_WIDTH + POOL_WIDTH + 2 * D_MODEL

kernel_name = 'hybrid_gqa_axial_rope_multiscale_pool_macaron_encoder'


def rmsnorm(x, g):
    xf = x.astype(jnp.float32)
    y = xf * lax.rsqrt(jnp.mean(xf * xf, axis=-1, keepdims=True) + EPS)
    return (y * g.astype(jnp.float32)).astype(x.dtype)


def swiglu(x, w_gate, w_up, w_down):
    return (jax.nn.silu(x @ w_gate) * (x @ w_up)) @ w_down


def axial_rope_tables(T):
    rows = T // GRID_W
    row = jnp.repeat(jnp.arange(rows, dtype=jnp.float32), GRID_W)
    col = jnp.broadcast_to(jnp.arange(GRID_W, dtype=jnp.float32)[None, :], (rows, GRID_W)).reshape(-1)
    half = HEAD_DIM // 2
    inv = ROPE_THETA ** (-jnp.arange(0, half, 2, dtype=jnp.float32) / half)
    ang = jnp.stack([row[:, None] * inv, col[:, None] * inv], axis=1)
    return jnp.cos(ang), jnp.sin(ang)


def apply_rope(x, cos, sin):
    B, T, H, _ = x.shape
    xr = x.reshape(B, T, H, 2, 2, HEAD_DIM // 4)
    x1 = xr[..., 0, :]
    x2 = xr[..., 1, :]
    c = cos[None, :, None].astype(x.dtype)
    s = sin[None, :, None].astype(x.dtype)
    out = jnp.stack([x1 * c - x2 * s, x2 * c + x1 * s], axis=-2)
    return out.reshape(B, T, H, HEAD_DIM)


def blocked_gqa(q, k, v):
    B, T = q.shape[:2]
    nblk = T // Q_BLOCK
    G = N_HEADS // N_KV_HEADS
    qb = q.reshape(B, nblk, Q_BLOCK, N_KV_HEADS, G, HEAD_DIM).swapaxes(0, 1)
    scale = HEAD_DIM ** -0.5

    def block(qi):
        s = jnp.einsum('bqkgd,bskd->bkgqs', qi, k).astype(jnp.float32) * scale
        p = jax.nn.softmax(s, axis=-1).astype(v.dtype)
        return jnp.einsum('bkgqs,bskd->bqkgd', p, v)

    o = lax.map(block, qb)
    return o.swapaxes(0, 1).reshape(B, T, ATTN_WIDTH)


def centred_mean_minus_self(z, w):
    T = z.shape[1]
    left = w // 2
    right = w - 1 - left
    zf = z.astype(jnp.float32)
    csum = jnp.concatenate([jnp.zeros_like(zf[:, :1]), jnp.cumsum(zf, axis=1)], axis=1)
    t = jnp.arange(T)
    lo = jnp.maximum(t - left, 0)
    hi = jnp.minimum(t + right, T - 1)
    total = jnp.take(csum, hi + 1, axis=1) - jnp.take(csum, lo, axis=1)
    cnt = (hi - lo + 1).astype(jnp.float32)[None, :, None]
    return (total / cnt - zf).astype(z.dtype)


def multiscale_pool(z, pool_w, pool_scale):
    B, T = z.shape[:2]
    zg = z.reshape(B, T, POOL_GROUPS, POOL_GROUP_WIDTH)
    pooled = jnp.stack([centred_mean_minus_self(zg[:, :, g], POOL_WINDOWS[g]) for g in range(POOL_GROUPS)], axis=2)
    mixed = jnp.einsum('btgc,gcd->btgd', pooled, pool_w).reshape(B, T, POOL_WIDTH)
    return mixed * pool_scale


def encoder_layer(x, p, ffn1_norm, ffn1_w_gate, ffn1_w_up, ffn1_w_down, mix_norm, w_in, q_norm, k_norm,
                  w_up_attn, pool_w, pool_scale, w_up_pool, w_out, ffn2_norm, ffn2_w_gate, ffn2_w_up,
                  ffn2_w_down, ple_gate_norm, w_ple_gate, w_ple_proj, ple_post_norm):
    B, T = x.shape[:2]
    h = x + 0.5 * swiglu(rmsnorm(x, ffn1_norm), ffn1_w_gate, ffn1_w_up, ffn1_w_down)
    u = rmsnorm(h, mix_norm)
    proj = u @ w_in
    q, k, v, z, ga, gb = jnp.split(proj, SPLITS, axis=-1)
    q = rmsnorm(q.reshape(B, T, N_HEADS, HEAD_DIM), q_norm)
    k = rmsnorm(k.reshape(B, T, N_KV_HEADS, HEAD_DIM), k_norm)
    v = v.reshape(B, T, N_KV_HEADS, HEAD_DIM)
    cos, sin = axial_rope_tables(T)
    q = apply_rope(q, cos, sin)
    k = apply_rope(k, cos, sin)
    a = blocked_gqa(q, k, v) @ w_up_attn
    b = multiscale_pool(z, pool_w, pool_scale) @ w_up_pool
    m = jax.nn.sigmoid(ga) * a + jax.nn.sigmoid(gb) * b
    h = h + m @ w_out
    h = h + 0.5 * swiglu(rmsnorm(h, ffn2_norm), ffn2_w_gate, ffn2_w_up, ffn2_w_down)
    gate = jax.nn.sigmoid(rmsnorm(h, ple_gate_norm) @ w_ple_gate)
    e = rmsnorm(p @ w_ple_proj, ple_post_norm)
    return h + gate * e


def setup_inputs(seed: int = 0) -> dict:
    key = jax.random.key(seed)
    ks = jax.random.split(key, 32)
    f32 = jnp.float32

    def nrm(k, shape, fan_in):
        return jax.random.normal(k, shape, f32) * (fan_in ** -0.5)

    def gain(k, shape):
        return 1.0 + 0.05 * jax.random.normal(k, shape, f32)

    L = DEPTH
    return {
        'x_prompt': jax.random.normal(ks[0], (BATCH, SEQ, D_MODEL), f32),
        'x_sample': jax.random.normal(ks[1], (DEC_BATCH, DEC_SEQ, D_MODEL), f32),
        'p_prompt': jax.random.normal(ks[2], (DEPTH, BATCH, SEQ, PLE_DIM), f32),
        'p_sample': jax.random.normal(ks[3], (DEPTH, DEC_BATCH, DEC_SEQ, PLE_DIM), f32),
        'ffn1_norm': gain(ks[4], (L, D_MODEL)),
        'ffn1_w_gate': nrm(ks[5], (L, D_MODEL, D_FF), D_MODEL),
        'ffn1_w_up': nrm(ks[6], (L, D_MODEL, D_FF), D_MODEL),
        'ffn1_w_down': nrm(ks[7], (L, D_FF, D_MODEL), D_FF),
        'mix_norm': gain(ks[8], (L, D_MODEL)),
        'w_in': nrm(ks[9], (L, D_MODEL, IN_WIDTH), D_MODEL),
        'q_norm': gain(ks[10], (L, HEAD_DIM)),
        'k_norm': gain(ks[11], (L, HEAD_DIM)),
        'w_up_attn': nrm(ks[12], (L, ATTN_WIDTH, D_MODEL), ATTN_WIDTH),
        'pool_w': nrm(ks[13], (L, POOL_GROUPS, POOL_GROUP_WIDTH, POOL_GROUP_WIDTH), POOL_GROUP_WIDTH),
        'pool_scale': 1.0 + 0.1 * jax.random.normal(ks[14], (L, POOL_WIDTH), f32),
        'w_up_pool': nrm(ks[15], (L, POOL_WIDTH, D_MODEL), POOL_WIDTH),
        'w_out': nrm(ks[16], (L, D_MODEL, D_MODEL), D_MODEL),
        'ffn2_norm': gain(ks[17], (L, D_MODEL)),
        'ffn2_w_gate': nrm(ks[18], (L, D_MODEL, D_FF), D_MODEL),
        'ffn2_w_up': nrm(ks[19], (L, D_MODEL, D_FF), D_MODEL),
        'ffn2_w_down': nrm(ks[20], (L, D_FF, D_MODEL), D_FF),
        'ple_gate_norm': gain(ks[21], (L, D_MODEL)),
        'w_ple_gate': nrm(ks[22], (L, D_MODEL, D_MODEL), D_MODEL),
        'w_ple_proj': nrm(ks[23], (L, PLE_DIM, D_MODEL), PLE_DIM),
        'ple_post_norm': gain(ks[24], (L, D_MODEL)),
    }


def reference(x_prompt, x_sample, p_prompt, p_sample, ffn1_norm, ffn1_w_gate, ffn1_w_up, ffn1_w_down,
              mix_norm, w_in, q_norm, k_norm, w_up_attn, pool_w, pool_scale, w_up_pool, w_out,
              ffn2_norm, ffn2_w_gate, ffn2_w_up, ffn2_w_down, ple_gate_norm, w_ple_gate, w_ple_proj,
              ple_post_norm):
    weights = (ffn1_norm, ffn1_w_gate, ffn1_w_up, ffn1_w_down, mix_norm, w_in, q_norm, k_norm,
               w_up_attn, pool_w, pool_scale, w_up_pool, w_out, ffn2_norm, ffn2_w_gate, ffn2_w_up,
               ffn2_w_down, ple_gate_norm, w_ple_gate, w_ple_proj, ple_post_norm)

    def trunk(x, p):
        h = x
        for i in range(DEPTH):
            h = encoder_layer(h, p[i], *[w[i] for w in weights])
        return h

    y_prompt = trunk(x_prompt, p_prompt)
    y_sample = trunk(x_sample, p_sample)
    return (y_prompt, y_sample)
```

```python
import functools

import jax
import jax.numpy as jnp
from jax import lax
from jax.experimental import pallas as pl
from jax.experimental.pallas import tpu as pltpu

F32 = jnp.float32
BF16 = jnp.bfloat16

D_MODEL = 1024
GRID_W = 64
N_HEADS = 8
N_KV_HEADS = 2
GROUP = N_HEADS // N_KV_HEADS
HEAD_DIM = 64
ATTN_WIDTH = N_HEADS * HEAD_DIM
KV_WIDTH = N_KV_HEADS * HEAD_DIM
QKV_WIDTH = ATTN_WIDTH + 2 * KV_WIDTH
POOL_WINDOWS = (2, 4, 8, 16)
POOL_WIDTH = D_MODEL // 2
POOL_GROUP_WIDTH = POOL_WIDTH // len(POOL_WINDOWS)
POOL_HALO = 8
D_FF = 2816
FF_CHUNK = 256
PLE_DIM = 256
ROPE_THETA = 10000.0
EPS = 1e-6
SM_SCALE = HEAD_DIM ** -0.5

V7X_LANES = 128
VMEM_LIMIT_BYTES = 60 * 1024 * 1024

NT_DIMS = (((1,), (1,)), ((), ()))


def _dot(a, b):
    return jnp.dot(a, b, preferred_element_type=F32)


def _sigmoid(x):
    return 1.0 / (1.0 + jnp.exp(-x))


def _rmsnorm(x, g):
    ms = jnp.mean(x * x, axis=-1, keepdims=True)
    return x * lax.rsqrt(ms + EPS) * g


def _swiglu(xn, wg_ref, wu_ref, wd_ref, a_scr):
    for j in range(D_FF // FF_CHUNK):
        cols = slice(j * FF_CHUNK, (j + 1) * FF_CHUNK)
        g = _dot(xn, wg_ref[:, cols])
        u = _dot(xn, wu_ref[:, cols])
        a_scr[:, cols] = (g * _sigmoid(g) * u).astype(BF16)
    return _dot(a_scr[...], wd_ref[...])


def _head_norm_rope_t(blk, gain, cos, sin):
    ms = jnp.mean(blk * blk, axis=0, keepdims=True)
    y = blk * lax.rsqrt(ms + EPS) * gain
    q4 = HEAD_DIM // 4
    partner = jnp.concatenate([y[q4:2 * q4], y[:q4], y[3 * q4:], y[2 * q4:3 * q4]], axis=0)
    return y * cos + partner * sin


def _ffn1_proj_kernel(x_ref, n1_ref, wg_ref, wu_ref, wd_ref, nm_ref, wqkvt_ref, wzg_ref,
                      gq_ref, gk_ref, cos_ref, sin_ref,
                      h_ref, qt_ref, k_ref, vt_ref, z_ref, sg_ref, a_scr):
    x = x_ref[0]
    xn = _rmsnorm(x, n1_ref[...]).astype(BF16)
    h = x + 0.5 * _swiglu(xn, wg_ref, wu_ref, wd_ref, a_scr)
    h_ref[0] = h

    un = _rmsnorm(h, nm_ref[...]).astype(BF16)
    qkvt = lax.dot_general(wqkvt_ref[...], un, NT_DIMS, preferred_element_type=F32)
    cos = cos_ref[...]
    sin = sin_ref[...]
    gq = gq_ref[...]
    gk = gk_ref[...]
    for hd in range(N_HEADS):
        blk = qkvt[hd * HEAD_DIM:(hd + 1) * HEAD_DIM]
        qh = (_head_norm_rope_t(blk, gq, cos, sin) * SM_SCALE).astype(BF16)
        slabs = [jnp.zeros_like(qh)] * N_KV_HEADS
        slabs[hd // GROUP] = qh
        qt_ref[0, hd * KV_WIDTH:(hd + 1) * KV_WIDTH, :] = jnp.concatenate(slabs, axis=0)
    kt = jnp.concatenate(
        [_head_norm_rope_t(qkvt[ATTN_WIDTH + g * HEAD_DIM:ATTN_WIDTH + (g + 1) * HEAD_DIM], gk, cos, sin)
         for g in range(N_KV_HEADS)], axis=0)
    k_ref[0] = kt.T.astype(BF16)
    vt_ref[0] = qkvt[ATTN_WIDTH + KV_WIDTH:].astype(BF16)

    zg = _dot(un, wzg_ref[...])
    z_ref[0] = zg[:, :POOL_WIDTH]
    sg_ref[0] = _sigmoid(zg[:, POOL_WIDTH:]).astype(BF16)


def _attn_kernel(qt_ref, k_ref, vt_ref, o_ref, m_scr, l_scr, acc_scr):
    ki = pl.program_id(2)

    @pl.when(ki == 0)
    def _():
        m_scr[...] = jnp.full(m_scr.shape, -jnp.inf, F32)
        l_scr[...] = jnp.zeros(l_scr.shape, F32)
        acc_scr[...] = jnp.zeros(acc_scr.shape, F32)

    k = k_ref[0]
    vt = vt_ref[0]
    for hd in range(N_HEADS):
        g = hd // GROUP
        rows = slice(hd * HEAD_DIM, (hd + 1) * HEAD_DIM)
        st = _dot(k, qt_ref[0, hd * KV_WIDTH:(hd + 1) * KV_WIDTH, :])
        m_prev = m_scr[hd:hd + 1, :]
        m_new = jnp.maximum(m_prev, jnp.max(st, axis=0, keepdims=True))
        alpha = jnp.exp(m_prev - m_new)
        pt = jnp.exp(st - m_new)
        l_scr[hd:hd + 1, :] = alpha * l_scr[hd:hd + 1, :] + jnp.sum(pt, axis=0, keepdims=True)
        pv = _dot(vt[g * HEAD_DIM:(g + 1) * HEAD_DIM], pt.astype(BF16))
        acc_scr[rows, :] = alpha * acc_scr[rows, :] + pv
        m_scr[hd:hd + 1, :] = m_new

    @pl.when(ki == pl.num_programs(2) - 1)
    def _():
        ot = jnp.concatenate(
            [acc_scr[hd * HEAD_DIM:(hd + 1) * HEAD_DIM, :] / l_scr[hd:hd + 1, :] for hd in range(N_HEADS)],
            axis=0)
        o_ref[0] = ot.T.astype(BF16)


def _pooled(zext_scr, tm, t0, seq_len):
    t = t0 + lax.broadcasted_iota(jnp.int32, (tm, POOL_GROUP_WIDTH), 0)
    parts = []
    for g, w in enumerate(POOL_WINDOWS):
        left = w // 2
        right = w - 1 - left
        cols = slice(g * POOL_GROUP_WIDTH, (g + 1) * POOL_GROUP_WIDTH)
        total = zext_scr[POOL_HALO - left:POOL_HALO - left + tm, cols]
        for j in range(-left + 1, right + 1):
            total = total + zext_scr[POOL_HALO + j:POOL_HALO + j + tm, cols]
        cnt = (jnp.minimum(t + right, seq_len - 1) - jnp.maximum(t - left, 0) + 1).astype(F32)
        parts.append(total / cnt - zext_scr[POOL_HALO:POOL_HALO + tm, cols])
    return jnp.concatenate(parts, axis=1)


def _mix_ffn2_ple_kernel(h_ref, o_ref, z_ref, zprev_ref, znext_ref, sg_ref, p_ref,
                         wua_ref, wpool_ref, pscale_ref, wup_ref, wout_ref,
                         n2_ref, wg_ref, wu_ref, wd_ref,
                         npg_ref, wpg_ref, wpp_ref, npp_ref,
                         y_ref, zext_scr, a_scr, *, seq_len):
    i = pl.program_id(1)
    tm = h_ref.shape[1]
    h = h_ref[0]

    zext_scr[:POOL_HALO, :] = jnp.where(i > 0, zprev_ref[0], 0.0)
    zext_scr[POOL_HALO:POOL_HALO + tm, :] = z_ref[0]
    zext_scr[POOL_HALO + tm:, :] = jnp.where(i < pl.num_programs(1) - 1, znext_ref[0], 0.0)
    pooled = _pooled(zext_scr, tm, i * tm, seq_len).astype(BF16)
    mixed = _dot(pooled, wpool_ref[...]) * pscale_ref[...]
    b = _dot(mixed.astype(BF16), wup_ref[...])
    a = _dot(o_ref[0], wua_ref[...])
    sg = sg_ref[0]
    m = sg[:, :D_MODEL].astype(F32) * a + sg[:, D_MODEL:].astype(F32) * b
    h = h + _dot(m.astype(BF16), wout_ref[...])

    xn = _rmsnorm(h, n2_ref[...]).astype(BF16)
    h = h + 0.5 * _swiglu(xn, wg_ref, wu_ref, wd_ref, a_scr)

    gate = _sigmoid(_dot(_rmsnorm(h, npg_ref[...]).astype(BF16), wpg_ref[...]))
    e = _rmsnorm(_dot(p_ref[0].astype(BF16), wpp_ref[...]), npp_ref[...])
    y_ref[0] = h + gate * e


def _tile(n, candidates=(512, 256, 128)):
    for c in candidates:
        if n % c == 0:
            return c
    raise ValueError(f"sequence length {n} must be a multiple of {candidates[-1]}")


def _resident(shape):
    return pl.BlockSpec(shape, lambda *_: (0,) * len(shape), pipeline_mode=pl.Buffered(1))


def _rope_tables(seq_len):
    rows = seq_len // GRID_W
    row = jnp.repeat(jnp.arange(rows, dtype=F32), GRID_W)
    col = jnp.tile(jnp.arange(GRID_W, dtype=F32), rows)
    half = HEAD_DIM // 2
    inv = ROPE_THETA ** (-jnp.arange(0, half, 2, dtype=F32) / half)
    ang_r = (row[:, None] * inv).T
    ang_c = (col[:, None] * inv).T
    cos = jnp.concatenate([jnp.cos(ang_r)] * 2 + [jnp.cos(ang_c)] * 2, axis=0)
    sin = jnp.concatenate([-jnp.sin(ang_r), jnp.sin(ang_r), -jnp.sin(ang_c), jnp.sin(ang_c)], axis=0)
    return cos, sin


def _prep_layer(w):
    (ffn1_norm, ffn1_w_gate, ffn1_w_up, ffn1_w_down, mix_norm, w_in, q_norm, k_norm,
     w_up_attn, pool_w, pool_scale, w_up_pool, w_out, ffn2_norm, ffn2_w_gate, ffn2_w_up,
     ffn2_w_down, ple_gate_norm, w_ple_gate, w_ple_proj, ple_post_norm) = w
    row = lambda v: v.reshape(1, -1).astype(F32)
    pool_bd = jnp.zeros((POOL_WIDTH, POOL_WIDTH), F32)
    for g in range(len(POOL_WINDOWS)):
        s = slice(g * POOL_GROUP_WIDTH, (g + 1) * POOL_GROUP_WIDTH)
        pool_bd = pool_bd.at[s, s].set(pool_w[g])
    return dict(
        n1=row(ffn1_norm), wg1=ffn1_w_gate.astype(BF16), wu1=ffn1_w_up.astype(BF16),
        wd1=ffn1_w_down.astype(BF16), nm=row(mix_norm),
        wqkvt=w_in[:, :QKV_WIDTH].T.astype(BF16), wzg=w_in[:, QKV_WIDTH:].astype(BF16),
        q_norm=q_norm.astype(F32), k_norm=k_norm.astype(F32),
        wua=w_up_attn.astype(BF16), wpool=pool_bd.astype(BF16), pscale=row(pool_scale),
        wup=w_up_pool.astype(BF16), wout=w_out.astype(BF16),
        n2=row(ffn2_norm), wg2=ffn2_w_gate.astype(BF16), wu2=ffn2_w_up.astype(BF16),
        wd2=ffn2_w_down.astype(BF16),
        npg=row(ple_gate_norm), wpg=w_ple_gate.astype(BF16), wpp=w_ple_proj.astype(BF16),
        npp=row(ple_post_norm))


def _ffn1_proj(x, lw, cos, sin):
    B, T, D = x.shape
    tm = _tile(T)
    nt = T // tm
    gq = jnp.broadcast_to(lw["q_norm"][:, None], (HEAD_DIM, tm))
    gk = jnp.broadcast_to(lw["k_norm"][:, None], (HEAD_DIM, tm))
    tile = lambda w: pl.BlockSpec((1, tm, w), lambda b, i: (b, i, 0))
    tile_t = lambda r: pl.BlockSpec((1, r, tm), lambda b, i: (b, 0, i))
    table = pl.BlockSpec((HEAD_DIM, tm), lambda b, i: (0, i))
    return pl.pallas_call(
        _ffn1_proj_kernel,
        grid=(B, nt),
        in_specs=[tile(D), _resident((1, D)), _resident((D, D_FF)), _resident((D, D_FF)),
                  _resident((D_FF, D)), _resident((1, D)), _resident((QKV_WIDTH, D)),
                  _resident((D, POOL_WIDTH + 2 * D)), _resident((HEAD_DIM, tm)),
                  _resident((HEAD_DIM, tm)), table, table],
        out_specs=[tile(D), tile_t(N_HEADS * KV_WIDTH), tile(KV_WIDTH), tile_t(KV_WIDTH),
                   tile(POOL_WIDTH), tile(2 * D)],
        out_shape=[jax.ShapeDtypeStruct((B, T, D), F32),
                   jax.ShapeDtypeStruct((B, N_HEADS * KV_WIDTH, T), BF16),
                   jax.ShapeDtypeStruct((B, T, KV_WIDTH), BF16),
                   jax.ShapeDtypeStruct((B, KV_WIDTH, T), BF16),
                   jax.ShapeDtypeStruct((B, T, POOL_WIDTH), F32),
                   jax.ShapeDtypeStruct((B, T, 2 * D), BF16)],
        scratch_shapes=[pltpu.VMEM((tm, D_FF), BF16)],
        compiler_params=pltpu.CompilerParams(
            dimension_semantics=("parallel", "parallel"), vmem_limit_bytes=VMEM_LIMIT_BYTES),
        name="ffn1_proj",
    )(x, lw["n1"], lw["wg1"], lw["wu1"], lw["wd1"], lw["nm"], lw["wqkvt"], lw["wzg"],
      gq, gk, cos, sin)


def _attention(qt, k, vt):
    B, T, _ = k.shape
    tq = _tile(T)
    tk = _tile(T)
    return pl.pallas_call(
        _attn_kernel,
        grid=(B, T // tq, T // tk),
        in_specs=[pl.BlockSpec((1, N_HEADS * KV_WIDTH, tq), lambda b, qi, ki: (b, 0, qi)),
                  pl.BlockSpec((1, tk, KV_WIDTH), lambda b, qi, ki: (b, ki, 0)),
                  pl.BlockSpec((1, KV_WIDTH, tk), lambda b, qi, ki: (b, 0, ki))],
        out_specs=pl.BlockSpec((1, tq, ATTN_WIDTH), lambda b, qi, ki: (b, qi, 0)),
        out_shape=jax.ShapeDtypeStruct((B, T, ATTN_WIDTH), BF16),
        scratch_shapes=[pltpu.VMEM((N_HEADS, tq), F32), pltpu.VMEM((N_HEADS, tq), F32),
                        pltpu.VMEM((ATTN_WIDTH, tq), F32)],
        compiler_params=pltpu.CompilerParams(
            dimension_semantics=("parallel", "parallel", "arbitrary"),
            vmem_limit_bytes=VMEM_LIMIT_BYTES),
        name="attention",
    )(qt, k, vt)


def _mix_ffn2_ple(h, o, z, sg, p, lw):
    B, T, D = h.shape
    tm = _tile(T)
    nt = T // tm
    halo_blocks = tm // POOL_HALO
    tile = lambda w: pl.BlockSpec((1, tm, w), lambda b, i: (b, i, 0))
    prev = pl.BlockSpec((1, POOL_HALO, POOL_WIDTH),
                        lambda b, i: (b, jnp.maximum(i * halo_blocks - 1, 0), 0))
    nxt = pl.BlockSpec((1, POOL_HALO, POOL_WIDTH),
                       lambda b, i: (b, jnp.minimum((i + 1) * halo_blocks, T // POOL_HALO - 1), 0))
    return pl.pallas_call(
        functools.partial(_mix_ffn2_ple_kernel, seq_len=T),
        grid=(B, nt),
        in_specs=[tile(D), tile(ATTN_WIDTH), tile(POOL_WIDTH), prev, nxt, tile(2 * D), tile(PLE_DIM),
                  _resident((ATTN_WIDTH, D)), _resident((POOL_WIDTH, POOL_WIDTH)),
                  _resident((1, POOL_WIDTH)), _resident((POOL_WIDTH, D)), _resident((D, D)),
                  _resident((1, D)), _resident((D, D_FF)), _resident((D, D_FF)), _resident((D_FF, D)),
                  _resident((1, D)), _resident((D, D)), _resident((PLE_DIM, D)), _resident((1, D))],
        out_specs=tile(D),
        out_shape=jax.ShapeDtypeStruct((B, T, D), F32),
        scratch_shapes=[pltpu.VMEM((tm + 2 * POOL_HALO, POOL_WIDTH), F32),
                        pltpu.VMEM((tm, D_FF), BF16)],
        compiler_params=pltpu.CompilerParams(
            dimension_semantics=("parallel", "parallel"), vmem_limit_bytes=VMEM_LIMIT_BYTES),
        name="mix_ffn2_ple",
    )(h, o, z, z, z, sg, p,
      lw["wua"], lw["wpool"], lw["pscale"], lw["wup"], lw["wout"],
      lw["n2"], lw["wg2"], lw["wu2"], lw["wd2"],
      lw["npg"], lw["wpg"], lw["wpp"], lw["npp"])


def _encoder_layer(x, p, lw):
    cos, sin = _rope_tables(x.shape[1])
    h, qt, k, vt, z, sg = _ffn1_proj(x, lw, cos, sin)
    o = _attention(qt, k, vt)
    return _mix_ffn2_ple(h, o, z, sg, p, lw)


def kernel(x_prompt, x_sample, p_prompt, p_sample, ffn1_norm, ffn1_w_gate, ffn1_w_up, ffn1_w_down,
           mix_norm, w_in, q_norm, k_norm, w_up_attn, pool_w, pool_scale, w_up_pool, w_out,
           ffn2_norm, ffn2_w_gate, ffn2_w_up, ffn2_w_down, ple_gate_norm, w_ple_gate, w_ple_proj,
           ple_post_norm):
    weights = (ffn1_norm, ffn1_w_gate, ffn1_w_up, ffn1_w_down, mix_norm, w_in, q_norm, k_norm,
               w_up_attn, pool_w, pool_scale, w_up_pool, w_out, ffn2_norm, ffn2_w_gate, ffn2_w_up,
               ffn2_w_down, ple_gate_norm, w_ple_gate, w_ple_proj, ple_post_norm)
    depth = ffn1_norm.shape[0]
    layers = [_prep_layer([w[i] for w in weights]) for i in range(depth)]

    def trunk(x, p):
        h = x
        for i in range(depth):
            h = _encoder_layer(h, p[i], layers[i])
        return h

    return (trunk(x_prompt, p_prompt), trunk(x_sample, p_sample))
```

```python
import functools

import jax
import jax.numpy as jnp
from jax import lax
from jax.experimental import pallas as pl
from jax.experimental.pallas import tpu as pltpu

F32 = jnp.float32
BF16 = jnp.bfloat16

D_MODEL = 1024
GRID_W = 64
N_HEADS = 8
N_KV_HEADS = 2
GROUP = N_HEADS // N_KV_HEADS
HEAD_DIM = 64
ATTN_WIDTH = N_HEADS * HEAD_DIM
KV_WIDTH = N_KV_HEADS * HEAD_DIM
QKV_WIDTH = ATTN_WIDTH + 2 * KV_WIDTH
POOL_WINDOWS = (2, 4, 8, 16)
POOL_WIDTH = D_MODEL // 2
POOL_GROUP_WIDTH = POOL_WIDTH // len(POOL_WINDOWS)
POOL_HALO = 8
D_FF = 2816
FF_CHUNK = 256
PLE_DIM = 256
ROPE_THETA = 10000.0
EPS = 1e-6
SM_SCALE = HEAD_DIM ** -0.5
LOG2E = 1.4426950408889634
Q_SCALE = SM_SCALE * LOG2E
ONES_ROWS = 16
V_ROWS = HEAD_DIM + ONES_ROWS
ATTN_TILES = (1024, 512, 256, 128)
ATTN_STRIP = 512
ATTN_QK_LOOKAHEAD = 2

V7X_LANES = 128
VMEM_LIMIT_BYTES = 60 * 1024 * 1024

NT_DIMS = (((1,), (1,)), ((), ()))


def _dot(a, b):
    return jnp.dot(a, b, preferred_element_type=F32)


def _sigmoid(x):
    return 1.0 / (1.0 + jnp.exp(-x))


def _rmsnorm(x, g):
    ms = jnp.mean(x * x, axis=-1, keepdims=True)
    return x * lax.rsqrt(ms + EPS) * g


def _swiglu(xn, wg_ref, wu_ref, wd_ref, a_scr):
    for j in range(D_FF // FF_CHUNK):
        cols = slice(j * FF_CHUNK, (j + 1) * FF_CHUNK)
        g = _dot(xn, wg_ref[:, cols])
        u = _dot(xn, wu_ref[:, cols])
        a_scr[:, cols] = (g * _sigmoid(g) * u).astype(BF16)
    return _dot(a_scr[...], wd_ref[...])


def _head_norm_rope_t(blk, gain, cos, sin):
    ms = jnp.mean(blk * blk, axis=0, keepdims=True)
    y = blk * lax.rsqrt(ms + EPS) * gain
    q4 = HEAD_DIM // 4
    partner = jnp.concatenate([y[q4:2 * q4], y[:q4], y[3 * q4:], y[2 * q4:3 * q4]], axis=0)
    return y * cos + partner * sin


def _ffn1_proj_kernel(x_ref, n1_ref, wg_ref, wu_ref, wd_ref, nm_ref, wqkvt_ref, wzg_ref,
                      gq_ref, gk_ref, cos_ref, sin_ref,
                      h_ref, qt_ref, k_ref, vt_ref, z_ref, sg_ref, a_scr):
    x = x_ref[0]
    xn = _rmsnorm(x, n1_ref[...]).astype(BF16)
    h = x + 0.5 * _swiglu(xn, wg_ref, wu_ref, wd_ref, a_scr)
    h_ref[0] = h

    un = _rmsnorm(h, nm_ref[...]).astype(BF16)
    qkvt = lax.dot_general(wqkvt_ref[...], un, NT_DIMS, preferred_element_type=F32)
    cos = cos_ref[...]
    sin = sin_ref[...]
    gq = gq_ref[...]
    gk = gk_ref[...]
    for hd in range(N_HEADS):
        blk = qkvt[hd * HEAD_DIM:(hd + 1) * HEAD_DIM]
        qh = (_head_norm_rope_t(blk, gq, cos, sin) * Q_SCALE).astype(BF16)
        slabs = [jnp.zeros_like(qh)] * N_KV_HEADS
        slabs[hd // GROUP] = qh
        qt_ref[0, hd * KV_WIDTH:(hd + 1) * KV_WIDTH, :] = jnp.concatenate(slabs, axis=0)
    kt = jnp.concatenate(
        [_head_norm_rope_t(qkvt[ATTN_WIDTH + g * HEAD_DIM:ATTN_WIDTH + (g + 1) * HEAD_DIM], gk, cos, sin)
         for g in range(N_KV_HEADS)], axis=0)
    k_ref[0] = kt.T.astype(BF16)
    ones = jnp.ones((ONES_ROWS, qkvt.shape[1]), BF16)
    for g in range(N_KV_HEADS):
        v0 = ATTN_WIDTH + KV_WIDTH + g * HEAD_DIM
        vt_ref[0, g * V_ROWS:(g + 1) * V_ROWS, :] = jnp.concatenate(
            [qkvt[v0:v0 + HEAD_DIM].astype(BF16), ones], axis=0)

    zg = _dot(un, wzg_ref[...])
    z_ref[0] = zg[:, :POOL_WIDTH]
    sg_ref[0] = _sigmoid(zg[:, POOL_WIDTH:]).astype(BF16)


def _attn_kernel(qt_ref, k_ref, vt_ref, o_ref, m_scr, acc_scr):
    ki = pl.program_id(2)
    tq = qt_ref.shape[2]
    strip = min(ATTN_STRIP, tq)

    @pl.when(ki == 0)
    def _():
        m_scr[...] = jnp.full(m_scr.shape, -jnp.inf, F32)
        acc_scr[...] = jnp.zeros(acc_scr.shape, F32)

    k = k_ref[0]
    vt = vt_ref[0]
    units = [(hd, s) for hd in range(N_HEADS) for s in range(tq // strip)]

    def scores(unit):
        hd, s = unit
        return _dot(k, qt_ref[0, hd * KV_WIDTH:(hd + 1) * KV_WIDTH, s * strip:(s + 1) * strip])

    def accumulate(unit, st):
        hd, s = unit
        g = hd // GROUP
        cols = slice(s * strip, (s + 1) * strip)
        rows = slice(hd * V_ROWS, (hd + 1) * V_ROWS)
        m_prev = m_scr[hd:hd + 1, cols]
        m_new = jnp.maximum(m_prev, jnp.max(st, axis=0, keepdims=True))
        alpha = jnp.exp2(m_prev - m_new)
        pt = jnp.exp2(st - m_new).astype(BF16)
        acc_scr[rows, cols] = alpha * acc_scr[rows, cols] + _dot(vt[g * V_ROWS:(g + 1) * V_ROWS], pt)
        m_scr[hd:hd + 1, cols] = m_new

    pending = [scores(u) for u in units[:ATTN_QK_LOOKAHEAD]]
    for i, unit in enumerate(units):
        if i + ATTN_QK_LOOKAHEAD < len(units):
            pending.append(scores(units[i + ATTN_QK_LOOKAHEAD]))
        accumulate(unit, pending.pop(0))

    @pl.when(ki == pl.num_programs(2) - 1)
    def _():
        ot = jnp.concatenate(
            [acc_scr[hd * V_ROWS:hd * V_ROWS + HEAD_DIM, :] / acc_scr[hd * V_ROWS + HEAD_DIM:hd * V_ROWS + HEAD_DIM + 1, :]
             for hd in range(N_HEADS)], axis=0)
        o_ref[0] = ot.T.astype(BF16)


def _pooled(zext_scr, tm, t0, seq_len):
    t = t0 + lax.broadcasted_iota(jnp.int32, (tm, POOL_GROUP_WIDTH), 0)
    parts = []
    for g, w in enumerate(POOL_WINDOWS):
        left = w // 2
        right = w - 1 - left
        cols = slice(g * POOL_GROUP_WIDTH, (g + 1) * POOL_GROUP_WIDTH)
        total = zext_scr[POOL_HALO - left:POOL_HALO - left + tm, cols]
        for j in range(-left + 1, right + 1):
            total = total + zext_scr[POOL_HALO + j:POOL_HALO + j + tm, cols]
        cnt = (jnp.minimum(t + right, seq_len - 1) - jnp.maximum(t - left, 0) + 1).astype(F32)
        parts.append(total / cnt - zext_scr[POOL_HALO:POOL_HALO + tm, cols])
    return jnp.concatenate(parts, axis=1)


def _mix_ffn2_ple_kernel(h_ref, o_ref, z_ref, zprev_ref, znext_ref, sg_ref, p_ref,
                         wua_ref, wpool_ref, pscale_ref, wup_ref, wout_ref,
                         n2_ref, wg_ref, wu_ref, wd_ref,
                         npg_ref, wpg_ref, wpp_ref, npp_ref,
                         y_ref, zext_scr, a_scr, *, seq_len):
    i = pl.program_id(1)
    tm = h_ref.shape[1]
    h = h_ref[0]

    zext_scr[:POOL_HALO, :] = jnp.where(i > 0, zprev_ref[0], 0.0)
    zext_scr[POOL_HALO:POOL_HALO + tm, :] = z_ref[0]
    zext_scr[POOL_HALO + tm:, :] = jnp.where(i < pl.num_programs(1) - 1, znext_ref[0], 0.0)
    pooled = _pooled(zext_scr, tm, i * tm, seq_len).astype(BF16)
    mixed = _dot(pooled, wpool_ref[...]) * pscale_ref[...]
    b = _dot(mixed.astype(BF16), wup_ref[...])
    a = _dot(o_ref[0], wua_ref[...])
    sg = sg_ref[0]
    m = sg[:, :D_MODEL].astype(F32) * a + sg[:, D_MODEL:].astype(F32) * b
    h = h + _dot(m.astype(BF16), wout_ref[...])

    xn = _rmsnorm(h, n2_ref[...]).astype(BF16)
    h = h + 0.5 * _swiglu(xn, wg_ref, wu_ref, wd_ref, a_scr)

    gate = _sigmoid(_dot(_rmsnorm(h, npg_ref[...]).astype(BF16), wpg_ref[...]))
    e = _rmsnorm(_dot(p_ref[0].astype(BF16), wpp_ref[...]), npp_ref[...])
    y_ref[0] = h + gate * e


def _tile(n, candidates=(512, 256, 128)):
    for c in candidates:
        if n % c == 0:
            return c
    raise ValueError(f"sequence length {n} must be a multiple of {candidates[-1]}")


def _resident(shape):
    return pl.BlockSpec(shape, lambda *_: (0,) * len(shape), pipeline_mode=pl.Buffered(1))


def _rope_tables(seq_len):
    rows = seq_len // GRID_W
    row = jnp.repeat(jnp.arange(rows, dtype=F32), GRID_W)
    col = jnp.tile(jnp.arange(GRID_W, dtype=F32), rows)
    half = HEAD_DIM // 2
    inv = ROPE_THETA ** (-jnp.arange(0, half, 2, dtype=F32) / half)
    ang_r = (row[:, None] * inv).T
    ang_c = (col[:, None] * inv).T
    cos = jnp.concatenate([jnp.cos(ang_r)] * 2 + [jnp.cos(ang_c)] * 2, axis=0)
    sin = jnp.concatenate([-jnp.sin(ang_r), jnp.sin(ang_r), -jnp.sin(ang_c), jnp.sin(ang_c)], axis=0)
    return cos, sin


def _prep_layer(w):
    (ffn1_norm, ffn1_w_gate, ffn1_w_up, ffn1_w_down, mix_norm, w_in, q_norm, k_norm,
     w_up_attn, pool_w, pool_scale, w_up_pool, w_out, ffn2_norm, ffn2_w_gate, ffn2_w_up,
     ffn2_w_down, ple_gate_norm, w_ple_gate, w_ple_proj, ple_post_norm) = w
    row = lambda v: v.reshape(1, -1).astype(F32)
    pool_bd = jnp.zeros((POOL_WIDTH, POOL_WIDTH), F32)
    for g in range(len(POOL_WINDOWS)):
        s = slice(g * POOL_GROUP_WIDTH, (g + 1) * POOL_GROUP_WIDTH)
        pool_bd = pool_bd.at[s, s].set(pool_w[g])
    return dict(
        n1=row(ffn1_norm), wg1=ffn1_w_gate.astype(BF16), wu1=ffn1_w_up.astype(BF16),
        wd1=ffn1_w_down.astype(BF16), nm=row(mix_norm),
        wqkvt=w_in[:, :QKV_WIDTH].T.astype(BF16), wzg=w_in[:, QKV_WIDTH:].astype(BF16),
        q_norm=q_norm.astype(F32), k_norm=k_norm.astype(F32),
        wua=w_up_attn.astype(BF16), wpool=pool_bd.astype(BF16), pscale=row(pool_scale),
        wup=w_up_pool.astype(BF16), wout=w_out.astype(BF16),
        n2=row(ffn2_norm), wg2=ffn2_w_gate.astype(BF16), wu2=ffn2_w_up.astype(BF16),
        wd2=ffn2_w_down.astype(BF16),
        npg=row(ple_gate_norm), wpg=w_ple_gate.astype(BF16), wpp=w_ple_proj.astype(BF16),
        npp=row(ple_post_norm))


def _ffn1_proj(x, lw, cos, sin):
    B, T, D = x.shape
    tm = _tile(T)
    nt = T // tm
    gq = jnp.broadcast_to(lw["q_norm"][:, None], (HEAD_DIM, tm))
    gk = jnp.broadcast_to(lw["k_norm"][:, None], (HEAD_DIM, tm))
    tile = lambda w: pl.BlockSpec((1, tm, w), lambda b, i: (b, i, 0))
    tile_t = lambda r: pl.BlockSpec((1, r, tm), lambda b, i: (b, 0, i))
    table = pl.BlockSpec((HEAD_DIM, tm), lambda b, i: (0, i))
    return pl.pallas_call(
        _ffn1_proj_kernel,
        grid=(B, nt),
        in_specs=[tile(D), _resident((1, D)), _resident((D, D_FF)), _resident((D, D_FF)),
                  _resident((D_FF, D)), _resident((1, D)), _resident((QKV_WIDTH, D)),
                  _resident((D, POOL_WIDTH + 2 * D)), _resident((HEAD_DIM, tm)),
                  _resident((HEAD_DIM, tm)), table, table],
        out_specs=[tile(D), tile_t(N_HEADS * KV_WIDTH), tile(KV_WIDTH), tile_t(N_KV_HEADS * V_ROWS),
                   tile(POOL_WIDTH), tile(2 * D)],
        out_shape=[jax.ShapeDtypeStruct((B, T, D), F32),
                   jax.ShapeDtypeStruct((B, N_HEADS * KV_WIDTH, T), BF16),
                   jax.ShapeDtypeStruct((B, T, KV_WIDTH), BF16),
                   jax.ShapeDtypeStruct((B, N_KV_HEADS * V_ROWS, T), BF16),
                   jax.ShapeDtypeStruct((B, T, POOL_WIDTH), F32),
                   jax.ShapeDtypeStruct((B, T, 2 * D), BF16)],
        scratch_shapes=[pltpu.VMEM((tm, D_FF), BF16)],
        compiler_params=pltpu.CompilerParams(
            dimension_semantics=("parallel", "parallel"), vmem_limit_bytes=VMEM_LIMIT_BYTES),
        name="ffn1_proj",
    )(x, lw["n1"], lw["wg1"], lw["wu1"], lw["wd1"], lw["nm"], lw["wqkvt"], lw["wzg"],
      gq, gk, cos, sin)


def _attention(qt, k, vt):
    B, T, _ = k.shape
    tq = _tile(T, ATTN_TILES)
    tk = _tile(T, ATTN_TILES)
    return pl.pallas_call(
        _attn_kernel,
        grid=(B, T // tq, T // tk),
        in_specs=[pl.BlockSpec((1, N_HEADS * KV_WIDTH, tq), lambda b, qi, ki: (b, 0, qi)),
                  pl.BlockSpec((1, tk, KV_WIDTH), lambda b, qi, ki: (b, ki, 0)),
                  pl.BlockSpec((1, N_KV_HEADS * V_ROWS, tk), lambda b, qi, ki: (b, 0, ki))],
        out_specs=pl.BlockSpec((1, tq, ATTN_WIDTH), lambda b, qi, ki: (b, qi, 0)),
        out_shape=jax.ShapeDtypeStruct((B, T, ATTN_WIDTH), BF16),
        scratch_shapes=[pltpu.VMEM((N_HEADS, tq), F32), pltpu.VMEM((N_HEADS * V_ROWS, tq), F32)],
        compiler_params=pltpu.CompilerParams(
            dimension_semantics=("parallel", "parallel", "arbitrary"),
            vmem_limit_bytes=VMEM_LIMIT_BYTES),
        name="attention",
    )(qt, k, vt)


def _mix_ffn2_ple(h, o, z, sg, p, lw):
    B, T, D = h.shape
    tm = _tile(T)
    nt = T // tm
    halo_blocks = tm // POOL_HALO
    tile = lambda w: pl.BlockSpec((1, tm, w), lambda b, i: (b, i, 0))
    prev = pl.BlockSpec((1, POOL_HALO, POOL_WIDTH),
                        lambda b, i: (b, jnp.maximum(i * halo_blocks - 1, 0), 0))
    nxt = pl.BlockSpec((1, POOL_HALO, POOL_WIDTH),
                       lambda b, i: (b, jnp.minimum((i + 1) * halo_blocks, T // POOL_HALO - 1), 0))
    return pl.pallas_call(
        functools.partial(_mix_ffn2_ple_kernel, seq_len=T),
        grid=(B, nt),
        in_specs=[tile(D), tile(ATTN_WIDTH), tile(POOL_WIDTH), prev, nxt, tile(2 * D), tile(PLE_DIM),
                  _resident((ATTN_WIDTH, D)), _resident((POOL_WIDTH, POOL_WIDTH)),
                  _resident((1, POOL_WIDTH)), _resident((POOL_WIDTH, D)), _resident((D, D)),
                  _resident((1, D)), _resident((D, D_FF)), _resident((D, D_FF)), _resident((D_FF, D)),
                  _resident((1, D)), _resident((D, D)), _resident((PLE_DIM, D)), _resident((1, D))],
        out_specs=tile(D),
        out_shape=jax.ShapeDtypeStruct((B, T, D), F32),
        scratch_shapes=[pltpu.VMEM((tm + 2 * POOL_HALO, POOL_WIDTH), F32),
                        pltpu.VMEM((tm, D_FF), BF16)],
        compiler_params=pltpu.CompilerParams(
            dimension_semantics=("parallel", "parallel"), vmem_limit_bytes=VMEM_LIMIT_BYTES),
        name="mix_ffn2_ple",
    )(h, o, z, z, z, sg, p,
      lw["wua"], lw["wpool"], lw["pscale"], lw["wup"], lw["wout"],
      lw["n2"], lw["wg2"], lw["wu2"], lw["wd2"],
      lw["npg"], lw["wpg"], lw["wpp"], lw["npp"])


def _encoder_layer(x, p, lw):
    cos, sin = _rope_tables(x.shape[1])
    h, qt, k, vt, z, sg = _ffn1_proj(x, lw, cos, sin)
    o = _attention(qt, k, vt)
    return _mix_ffn2_ple(h, o, z, sg, p, lw)


def kernel(x_prompt, x_sample, p_prompt, p_sample, ffn1_norm, ffn1_w_gate, ffn1_w_up, ffn1_w_down,
           mix_norm, w_in, q_norm, k_norm, w_up_attn, pool_w, pool_scale, w_up_pool, w_out,
           ffn2_norm, ffn2_w_gate, ffn2_w_up, ffn2_w_down, ple_gate_norm, w_ple_gate, w_ple_proj,
           ple_post_norm):
    weights = (ffn1_norm, ffn1_w_gate, ffn1_w_up, ffn1_w_down, mix_norm, w_in, q_norm, k_norm,
               w_up_attn, pool_w, pool_scale, w_up_pool, w_out, ffn2_norm, ffn2_w_gate, ffn2_w_up,
               ffn2_w_down, ple_gate_norm, w_ple_gate, w_ple_proj, ple_post_norm)
    depth = ffn1_norm.shape[0]
    layers = [_prep_layer([w[i] for w in weights]) for i in range(depth)]

    def trunk(x, p):
        h = x
        for i in range(depth):
            h = _encoder_layer(h, p[i], layers[i])
        return h

    return (trunk(x_prompt, p_prompt), trunk(x_sample, p_sample))
```

```python
import functools

import jax
import jax.numpy as jnp
from jax import lax
from jax.experimental import pallas as pl
from jax.experimental.pallas import tpu as pltpu

F32 = jnp.float32
BF16 = jnp.bfloat16

D_MODEL = 1024
GRID_W = 64
N_HEADS = 8
N_KV_HEADS = 2
GROUP = N_HEADS // N_KV_HEADS
HEAD_DIM = 64
ATTN_WIDTH = N_HEADS * HEAD_DIM
KV_WIDTH = N_KV_HEADS * HEAD_DIM
QKV_WIDTH = ATTN_WIDTH + 2 * KV_WIDTH
POOL_WINDOWS = (2, 4, 8, 16)
POOL_WIDTH = D_MODEL // 2
POOL_GROUP_WIDTH = POOL_WIDTH // len(POOL_WINDOWS)
POOL_HALO = 8
D_FF = 2816
FF_CHUNK = 256
N_SUB = 2
PLE_DIM = 256
ROPE_THETA = 10000.0
EPS = 1e-6
SM_SCALE = HEAD_DIM ** -0.5
LOG2E = 1.4426950408889634
Q_SCALE = SM_SCALE * LOG2E
ONES_ROWS = 16
V_ROWS = HEAD_DIM + ONES_ROWS
ATTN_TILES = (1024, 512, 256, 128)
ATTN_STRIP = 512
ATTN_QK_LOOKAHEAD = 2

V7X_LANES = 128
VMEM_LIMIT_BYTES = 60 * 1024 * 1024

NT_DIMS = (((1,), (1,)), ((), ()))


def _dot(a, b):
    return jnp.dot(a, b, preferred_element_type=F32)


def _sigmoid(x):
    return 1.0 / (1.0 + jnp.exp(-x))


def _rmsnorm(x, g):
    ms = jnp.mean(x * x, axis=-1, keepdims=True)
    return x * lax.rsqrt(ms + EPS) * g


def _row_slices(tm):
    sub = tm // N_SUB
    return [slice(s * sub, (s + 1) * sub) for s in range(N_SUB)]


def _swiglu(xns, rows, wg_ref, wu_ref, wd_ref, a_scr):
    for c0 in range(0, D_FF, FF_CHUNK):
        cols = slice(c0, min(c0 + FF_CHUNK, D_FF))
        for xn, r in zip(xns, rows):
            g = _dot(xn, wg_ref[:, cols])
            u = _dot(xn, wu_ref[:, cols])
            a_scr[r, cols] = (g * _sigmoid(g) * u).astype(BF16)
    return [_dot(a_scr[r, :], wd_ref[...]) for r in rows]


def _head_norm_rope_t(blk, gain, cos, sin):
    ms = jnp.mean(blk * blk, axis=0, keepdims=True)
    y = blk * lax.rsqrt(ms + EPS) * gain
    q4 = HEAD_DIM // 4
    partner = jnp.concatenate([y[q4:2 * q4], y[:q4], y[3 * q4:], y[2 * q4:3 * q4]], axis=0)
    return y * cos + partner * sin


def _ffn1_proj_kernel(x_ref, n1_ref, wg_ref, wu_ref, wd_ref, nm_ref, wqkvt_ref, wzg_ref,
                      gq_ref, gk_ref, cos_ref, sin_ref,
                      h_ref, qt_ref, k_ref, vt_ref, z_ref, sg_ref, a_scr):
    rows = _row_slices(x_ref.shape[1])
    xs = [x_ref[0, r, :] for r in rows]
    xns = [_rmsnorm(x, n1_ref[...]).astype(BF16) for x in xs]
    ys = _swiglu(xns, rows, wg_ref, wu_ref, wd_ref, a_scr)
    hs = [x + 0.5 * y for x, y in zip(xs, ys)]
    for r, h in zip(rows, hs):
        h_ref[0, r, :] = h

    uns = [_rmsnorm(h, nm_ref[...]).astype(BF16) for h in hs]
    qkvts = [lax.dot_general(wqkvt_ref[...], un, NT_DIMS, preferred_element_type=F32) for un in uns]
    zgs = [_dot(un, wzg_ref[...]) for un in uns]
    for r, qkvt, zg in zip(rows, qkvts, zgs):
        cos = cos_ref[:, r]
        sin = sin_ref[:, r]
        gq = gq_ref[:, r]
        gk = gk_ref[:, r]
        for hd in range(N_HEADS):
            blk = qkvt[hd * HEAD_DIM:(hd + 1) * HEAD_DIM]
            qh = (_head_norm_rope_t(blk, gq, cos, sin) * Q_SCALE).astype(BF16)
            slabs = [jnp.zeros_like(qh)] * N_KV_HEADS
            slabs[hd // GROUP] = qh
            qt_ref[0, hd * KV_WIDTH:(hd + 1) * KV_WIDTH, r] = jnp.concatenate(slabs, axis=0)
        kt = jnp.concatenate(
            [_head_norm_rope_t(qkvt[ATTN_WIDTH + g * HEAD_DIM:ATTN_WIDTH + (g + 1) * HEAD_DIM], gk, cos, sin)
             for g in range(N_KV_HEADS)], axis=0)
        k_ref[0, r, :] = kt.T.astype(BF16)
        ones = jnp.ones((ONES_ROWS, qkvt.shape[1]), BF16)
        for g in range(N_KV_HEADS):
            v0 = ATTN_WIDTH + KV_WIDTH + g * HEAD_DIM
            vt_ref[0, g * V_ROWS:(g + 1) * V_ROWS, r] = jnp.concatenate(
                [qkvt[v0:v0 + HEAD_DIM].astype(BF16), ones], axis=0)
        z_ref[0, r, :] = zg[:, :POOL_WIDTH]
        sg_ref[0, r, :] = _sigmoid(zg[:, POOL_WIDTH:]).astype(BF16)


def _attn_kernel(qt_ref, k_ref, vt_ref, o_ref, m_scr, acc_scr):
    ki = pl.program_id(2)
    tq = qt_ref.shape[2]
    strip = min(ATTN_STRIP, tq)

    @pl.when(ki == 0)
    def _():
        m_scr[...] = jnp.full(m_scr.shape, -jnp.inf, F32)
        acc_scr[...] = jnp.zeros(acc_scr.shape, F32)

    k = k_ref[0]
    vt = vt_ref[0]
    units = [(hd, s) for hd in range(N_HEADS) for s in range(tq // strip)]

    def scores(unit):
        hd, s = unit
        return _dot(k, qt_ref[0, hd * KV_WIDTH:(hd + 1) * KV_WIDTH, s * strip:(s + 1) * strip])

    def accumulate(unit, st):
        hd, s = unit
        g = hd // GROUP
        cols = slice(s * strip, (s + 1) * strip)
        rows = slice(hd * V_ROWS, (hd + 1) * V_ROWS)
        m_prev = m_scr[hd:hd + 1, cols]
        m_new = jnp.maximum(m_prev, jnp.max(st, axis=0, keepdims=True))
        alpha = jnp.exp2(m_prev - m_new)
        pt = jnp.exp2(st - m_new).astype(BF16)
        acc_scr[rows, cols] = alpha * acc_scr[rows, cols] + _dot(vt[g * V_ROWS:(g + 1) * V_ROWS], pt)
        m_scr[hd:hd + 1, cols] = m_new

    pending = [scores(u) for u in units[:ATTN_QK_LOOKAHEAD]]
    for i, unit in enumerate(units):
        if i + ATTN_QK_LOOKAHEAD < len(units):
            pending.append(scores(units[i + ATTN_QK_LOOKAHEAD]))
        accumulate(unit, pending.pop(0))

    @pl.when(ki == pl.num_programs(2) - 1)
    def _():
        ot = jnp.concatenate(
            [acc_scr[hd * V_ROWS:hd * V_ROWS + HEAD_DIM, :] / acc_scr[hd * V_ROWS + HEAD_DIM:hd * V_ROWS + HEAD_DIM + 1, :]
             for hd in range(N_HEADS)], axis=0)
        o_ref[0] = ot.T.astype(BF16)


def _pooled(zext_scr, r0, n, t0, seq_len):
    t = t0 + lax.broadcasted_iota(jnp.int32, (n, POOL_GROUP_WIDTH), 0)
    ext = n + 2 * POOL_HALO

    def ahead(v, d):
        return pltpu.roll(v, ext - d, 0)

    parts = []
    for g, w in enumerate(POOL_WINDOWS):
        left = w // 2
        right = w - 1 - left
        cols = slice(g * POOL_GROUP_WIDTH, (g + 1) * POOL_GROUP_WIDTH)
        zx = zext_scr[r0:r0 + ext, cols]
        fwd = zx
        span = 1
        while span < left:
            fwd = fwd + ahead(fwd, span)
            span *= 2
        total = (ahead(fwd, POOL_HALO - left) + ahead(fwd, POOL_HALO))[:n] if left < POOL_HALO \
            else (fwd + ahead(fwd, POOL_HALO))[:n]
        cnt = (jnp.minimum(t + right, seq_len - 1) - jnp.maximum(t - left, 0) + 1).astype(F32)
        parts.append(total / cnt - zx[POOL_HALO:POOL_HALO + n])
    return jnp.concatenate(parts, axis=1)


def _mix_ffn2_ple_kernel(h_ref, o_ref, z_ref, zprev_ref, znext_ref, sg_ref, p_ref,
                         wua_ref, wpool_ref, pscale_ref, wup_ref, wout_ref,
                         n2_ref, wg_ref, wu_ref, wd_ref,
                         npg_ref, wpg_ref, wpp_ref, npp_ref,
                         y_ref, zext_scr, a_scr, *, seq_len):
    i = pl.program_id(1)
    tm = h_ref.shape[1]
    rows = _row_slices(tm)

    zext_scr[:POOL_HALO, :] = jnp.where(i > 0, zprev_ref[0], 0.0)
    zext_scr[POOL_HALO:POOL_HALO + tm, :] = z_ref[0]
    zext_scr[POOL_HALO + tm:, :] = jnp.where(i < pl.num_programs(1) - 1, znext_ref[0], 0.0)

    avals = [_dot(o_ref[0, r, :], wua_ref[...]) for r in rows]
    eraw = [_dot(p_ref[0, r, :].astype(BF16), wpp_ref[...]) for r in rows]
    hs = []
    for r, a in zip(rows, avals):
        pooled = _pooled(zext_scr, r.start, r.stop - r.start, i * tm + r.start, seq_len).astype(BF16)
        mixed = _dot(pooled, wpool_ref[...]) * pscale_ref[...]
        b = _dot(mixed.astype(BF16), wup_ref[...])
        sg = sg_ref[0, r, :]
        m = sg[:, :D_MODEL].astype(F32) * a + sg[:, D_MODEL:].astype(F32) * b
        hs.append(h_ref[0, r, :] + _dot(m.astype(BF16), wout_ref[...]))

    xns = [_rmsnorm(h, n2_ref[...]).astype(BF16) for h in hs]
    ys = _swiglu(xns, rows, wg_ref, wu_ref, wd_ref, a_scr)
    hs = [h + 0.5 * y for h, y in zip(hs, ys)]

    es = [_rmsnorm(e, npp_ref[...]) for e in eraw]
    for r, h, e in zip(rows, hs, es):
        gate = _sigmoid(_dot(_rmsnorm(h, npg_ref[...]).astype(BF16), wpg_ref[...]))
        y_ref[0, r, :] = h + gate * e


def _tile(n, candidates=(512, 256, 128)):
    for c in candidates:
        if n % c == 0:
            return c
    raise ValueError(f"sequence length {n} must be a multiple of {candidates[-1]}")


def _resident(shape):
    return pl.BlockSpec(shape, lambda *_: (0,) * len(shape), pipeline_mode=pl.Buffered(1))


def _rope_tables(seq_len):
    rows = seq_len // GRID_W
    row = jnp.repeat(jnp.arange(rows, dtype=F32), GRID_W)
    col = jnp.tile(jnp.arange(GRID_W, dtype=F32), rows)
    half = HEAD_DIM // 2
    inv = ROPE_THETA ** (-jnp.arange(0, half, 2, dtype=F32) / half)
    ang_r = (row[:, None] * inv).T
    ang_c = (col[:, None] * inv).T
    cos = jnp.concatenate([jnp.cos(ang_r)] * 2 + [jnp.cos(ang_c)] * 2, axis=0)
    sin = jnp.concatenate([-jnp.sin(ang_r), jnp.sin(ang_r), -jnp.sin(ang_c), jnp.sin(ang_c)], axis=0)
    return cos, sin


def _prep_layer(w):
    (ffn1_norm, ffn1_w_gate, ffn1_w_up, ffn1_w_down, mix_norm, w_in, q_norm, k_norm,
     w_up_attn, pool_w, pool_scale, w_up_pool, w_out, ffn2_norm, ffn2_w_gate, ffn2_w_up,
     ffn2_w_down, ple_gate_norm, w_ple_gate, w_ple_proj, ple_post_norm) = w
    row = lambda v: v.reshape(1, -1).astype(F32)
    pool_bd = jnp.zeros((POOL_WIDTH, POOL_WIDTH), F32)
    for g in range(len(POOL_WINDOWS)):
        s = slice(g * POOL_GROUP_WIDTH, (g + 1) * POOL_GROUP_WIDTH)
        pool_bd = pool_bd.at[s, s].set(pool_w[g])
    return dict(
        n1=row(ffn1_norm), wg1=ffn1_w_gate.astype(BF16), wu1=ffn1_w_up.astype(BF16),
        wd1=ffn1_w_down.astype(BF16), nm=row(mix_norm),
        wqkvt=w_in[:, :QKV_WIDTH].T.astype(BF16), wzg=w_in[:, QKV_WIDTH:].astype(BF16),
        q_norm=q_norm.astype(F32), k_norm=k_norm.astype(F32),
        wua=w_up_attn.astype(BF16), wpool=pool_bd.astype(BF16), pscale=row(pool_scale),
        wup=w_up_pool.astype(BF16), wout=w_out.astype(BF16),
        n2=row(ffn2_norm), wg2=ffn2_w_gate.astype(BF16), wu2=ffn2_w_up.astype(BF16),
        wd2=ffn2_w_down.astype(BF16),
        npg=row(ple_gate_norm), wpg=w_ple_gate.astype(BF16), wpp=w_ple_proj.astype(BF16),
        npp=row(ple_post_norm))


def _ffn1_proj(x, lw, cos, sin):
    B, T, D = x.shape
    tm = _tile(T)
    nt = T // tm
    gq = jnp.broadcast_to(lw["q_norm"][:, None], (HEAD_DIM, tm))
    gk = jnp.broadcast_to(lw["k_norm"][:, None], (HEAD_DIM, tm))
    tile = lambda w: pl.BlockSpec((1, tm, w), lambda b, i: (b, i, 0))
    tile_t = lambda r: pl.BlockSpec((1, r, tm), lambda b, i: (b, 0, i))
    table = pl.BlockSpec((HEAD_DIM, tm), lambda b, i: (0, i))
    return pl.pallas_call(
        _ffn1_proj_kernel,
        grid=(B, nt),
        in_specs=[tile(D), _resident((1, D)), _resident((D, D_FF)), _resident((D, D_FF)),
                  _resident((D_FF, D)), _resident((1, D)), _resident((QKV_WIDTH, D)),
                  _resident((D, POOL_WIDTH + 2 * D)), _resident((HEAD_DIM, tm)),
                  _resident((HEAD_DIM, tm)), table, table],
        out_specs=[tile(D), tile_t(N_HEADS * KV_WIDTH), tile(KV_WIDTH), tile_t(N_KV_HEADS * V_ROWS),
                   tile(POOL_WIDTH), tile(2 * D)],
        out_shape=[jax.ShapeDtypeStruct((B, T, D), F32),
                   jax.ShapeDtypeStruct((B, N_HEADS * KV_WIDTH, T), BF16),
                   jax.ShapeDtypeStruct((B, T, KV_WIDTH), BF16),
                   jax.ShapeDtypeStruct((B, N_KV_HEADS * V_ROWS, T), BF16),
                   jax.ShapeDtypeStruct((B, T, POOL_WIDTH), F32),
                   jax.ShapeDtypeStruct((B, T, 2 * D), BF16)],
        scratch_shapes=[pltpu.VMEM((tm, D_FF), BF16)],
        compiler_params=pltpu.CompilerParams(
            dimension_semantics=("parallel", "parallel"), vmem_limit_bytes=VMEM_LIMIT_BYTES),
        name="ffn1_proj",
    )(x, lw["n1"], lw["wg1"], lw["wu1"], lw["wd1"], lw["nm"], lw["wqkvt"], lw["wzg"],
      gq, gk, cos, sin)


def _attention(qt, k, vt):
    B, T, _ = k.shape
    tq = _tile(T, ATTN_TILES)
    tk = _tile(T, ATTN_TILES)
    return pl.pallas_call(
        _attn_kernel,
        grid=(B, T // tq, T // tk),
        in_specs=[pl.BlockSpec((1, N_HEADS * KV_WIDTH, tq), lambda b, qi, ki: (b, 0, qi)),
                  pl.BlockSpec((1, tk, KV_WIDTH), lambda b, qi, ki: (b, ki, 0)),
                  pl.BlockSpec((1, N_KV_HEADS * V_ROWS, tk), lambda b, qi, ki: (b, 0, ki))],
        out_specs=pl.BlockSpec((1, tq, ATTN_WIDTH), lambda b, qi, ki: (b, qi, 0)),
        out_shape=jax.ShapeDtypeStruct((B, T, ATTN_WIDTH), BF16),
        scratch_shapes=[pltpu.VMEM((N_HEADS, tq), F32), pltpu.VMEM((N_HEADS * V_ROWS, tq), F32)],
        compiler_params=pltpu.CompilerParams(
            dimension_semantics=("parallel", "parallel", "arbitrary"),
            vmem_limit_bytes=VMEM_LIMIT_BYTES),
        name="attention",
    )(qt, k, vt)


def _mix_ffn2_ple(h, o, z, sg, p, lw):
    B, T, D = h.shape
    tm = _tile(T)
    nt = T // tm
    halo_blocks = tm // POOL_HALO
    tile = lambda w: pl.BlockSpec((1, tm, w), lambda b, i: (b, i, 0))
    prev = pl.BlockSpec((1, POOL_HALO, POOL_WIDTH),
                        lambda b, i: (b, jnp.maximum(i * halo_blocks - 1, 0), 0))
    nxt = pl.BlockSpec((1, POOL_HALO, POOL_WIDTH),
                       lambda b, i: (b, jnp.minimum((i + 1) * halo_blocks, T // POOL_HALO - 1), 0))
    return pl.pallas_call(
        functools.partial(_mix_ffn2_ple_kernel, seq_len=T),
        grid=(B, nt),
        in_specs=[tile(D), tile(ATTN_WIDTH), tile(POOL_WIDTH), prev, nxt, tile(2 * D), tile(PLE_DIM),
                  _resident((ATTN_WIDTH, D)), _resident((POOL_WIDTH, POOL_WIDTH)),
                  _resident((1, POOL_WIDTH)), _resident((POOL_WIDTH, D)), _resident((D, D)),
                  _resident((1, D)), _resident((D, D_FF)), _resident((D, D_FF)), _resident((D_FF, D)),
                  _resident((1, D)), _resident((D, D)), _resident((PLE_DIM, D)), _resident((1, D))],
        out_specs=tile(D),
        out_shape=jax.ShapeDtypeStruct((B, T, D), F32),
        scratch_shapes=[pltpu.VMEM((tm + 2 * POOL_HALO, POOL_WIDTH), F32),
                        pltpu.VMEM((tm, D_FF), BF16)],
        compiler_params=pltpu.CompilerParams(
            dimension_semantics=("parallel", "parallel"), vmem_limit_bytes=VMEM_LIMIT_BYTES),
        name="mix_ffn2_ple",
    )(h, o, z, z, z, sg, p,
      lw["wua"], lw["wpool"], lw["pscale"], lw["wup"], lw["wout"],
      lw["n2"], lw["wg2"], lw["wu2"], lw["wd2"],
      lw["npg"], lw["wpg"], lw["wpp"], lw["npp"])


def _encoder_layer(x, p, lw):
    cos, sin = _rope_tables(x.shape[1])
    h, qt, k, vt, z, sg = _ffn1_proj(x, lw, cos, sin)
    o = _attention(qt, k, vt)
    return _mix_ffn2_ple(h, o, z, sg, p, lw)


def kernel(x_prompt, x_sample, p_prompt, p_sample, ffn1_norm, ffn1_w_gate, ffn1_w_up, ffn1_w_down,
           mix_norm, w_in, q_norm, k_norm, w_up_attn, pool_w, pool_scale, w_up_pool, w_out,
           ffn2_norm, ffn2_w_gate, ffn2_w_up, ffn2_w_down, ple_gate_norm, w_ple_gate, w_ple_proj,
           ple_post_norm):
    weights = (ffn1_norm, ffn1_w_gate, ffn1_w_up, ffn1_w_down, mix_norm, w_in, q_norm, k_norm,
               w_up_attn, pool_w, pool_scale, w_up_pool, w_out, ffn2_norm, ffn2_w_gate, ffn2_w_up,
               ffn2_w_down, ple_gate_norm, w_ple_gate, w_ple_proj, ple_post_norm)
    depth = ffn1_norm.shape[0]
    layers = [_prep_layer([w[i] for w in weights]) for i in range(depth)]

    def trunk(x, p):
        h = x
        for i in range(depth):
            h = _encoder_layer(h, p[i], layers[i])
        return h

    return (trunk(x_prompt, p_prompt), trunk(x_sample, p_sample))
```

```python
import functools

import jax
import jax.numpy as jnp
from jax import lax
from jax.experimental import pallas as pl
from jax.experimental.pallas import tpu as pltpu

F32 = jnp.float32
BF16 = jnp.bfloat16

D_MODEL = 1024
GRID_W = 64
N_HEADS = 8
N_KV_HEADS = 2
GROUP = N_HEADS // N_KV_HEADS
HEAD_DIM = 64
ATTN_WIDTH = N_HEADS * HEAD_DIM
KV_WIDTH = N_KV_HEADS * HEAD_DIM
QKV_WIDTH = ATTN_WIDTH + 2 * KV_WIDTH
POOL_WINDOWS = (2, 4, 8, 16)
POOL_WIDTH = D_MODEL // 2
POOL_GROUP_WIDTH = POOL_WIDTH // len(POOL_WINDOWS)
POOL_HALO = 8
D_FF = 2816
FF_CHUNK = 256
N_SUB = 2
PLE_DIM = 256
ROPE_THETA = 10000.0
EPS = 1e-6
SM_SCALE = HEAD_DIM ** -0.5
LOG2E = 1.4426950408889634
Q_SCALE = SM_SCALE * LOG2E
ONES_ROWS = 16
V_ROWS = HEAD_DIM + ONES_ROWS
K_AUG = 256
REF_ROWS = 16
KSQ_ROWS = 8
ATTN_FIXED_REF_MARGIN = 60.0
ATTN_TILES = (1024, 512, 256, 128)
ATTN_STRIP = 512
ATTN_QK_LOOKAHEAD = 2

V7X_LANES = 128
VMEM_LIMIT_BYTES = 60 * 1024 * 1024

NT_DIMS = (((1,), (1,)), ((), ()))


def _dot(a, b):
    return jnp.dot(a, b, preferred_element_type=F32)


def _sigmoid(x):
    return 1.0 / (1.0 + jnp.exp(-x))


def _rmsnorm(x, g):
    ms = jnp.mean(x * x, axis=-1, keepdims=True)
    return x * lax.rsqrt(ms + EPS) * g


def _row_slices(tm):
    sub = tm // N_SUB
    return [slice(s * sub, (s + 1) * sub) for s in range(N_SUB)]


def _swiglu(xns, rows, wg_ref, wu_ref, wd_ref, a_scr):
    for c0 in range(0, D_FF, FF_CHUNK):
        cols = slice(c0, min(c0 + FF_CHUNK, D_FF))
        for xn, r in zip(xns, rows):
            g = _dot(xn, wg_ref[:, cols])
            u = _dot(xn, wu_ref[:, cols])
            a_scr[r, cols] = (g * _sigmoid(g) * u).astype(BF16)
    return [_dot(a_scr[r, :], wd_ref[...]) for r in rows]


def _head_norm_rope_t(blk, gain, cos, sin):
    ms = jnp.mean(blk * blk, axis=0, keepdims=True)
    y = blk * lax.rsqrt(ms + EPS) * gain
    q4 = HEAD_DIM // 4
    partner = jnp.concatenate([y[q4:2 * q4], y[:q4], y[3 * q4:], y[2 * q4:3 * q4]], axis=0)
    return y * cos + partner * sin


def _ffn1_proj_kernel(x_ref, n1_ref, wg_ref, wu_ref, wd_ref, nm_ref, wqkvt_ref, wzg_ref,
                      gq_ref, gk_ref, cos_ref, sin_ref,
                      h_ref, qt_ref, k_ref, vt_ref, z_ref, sg_ref, ksq_ref, a_scr):
    rows = _row_slices(x_ref.shape[1])
    xs = [x_ref[0, r, :] for r in rows]
    xns = [_rmsnorm(x, n1_ref[...]).astype(BF16) for x in xs]
    ys = _swiglu(xns, rows, wg_ref, wu_ref, wd_ref, a_scr)
    hs = [x + 0.5 * y for x, y in zip(xs, ys)]
    for r, h in zip(rows, hs):
        h_ref[0, r, :] = h

    uns = [_rmsnorm(h, nm_ref[...]).astype(BF16) for h in hs]
    qkvts = [lax.dot_general(wqkvt_ref[...], un, NT_DIMS, preferred_element_type=F32) for un in uns]
    zgs = [_dot(un, wzg_ref[...]) for un in uns]
    for r, qkvt, zg in zip(rows, qkvts, zgs):
        cos = cos_ref[:, r]
        sin = sin_ref[:, r]
        gq = gq_ref[:, r]
        gk = gk_ref[:, r]
        for hd in range(N_HEADS):
            blk = qkvt[hd * HEAD_DIM:(hd + 1) * HEAD_DIM]
            qh = (_head_norm_rope_t(blk, gq, cos, sin) * Q_SCALE).astype(BF16)
            slabs = [jnp.zeros_like(qh)] * N_KV_HEADS
            slabs[hd // GROUP] = qh
            qt_ref[0, hd * KV_WIDTH:(hd + 1) * KV_WIDTH, r] = jnp.concatenate(slabs, axis=0)
        kt = jnp.concatenate(
            [_head_norm_rope_t(qkvt[ATTN_WIDTH + g * HEAD_DIM:ATTN_WIDTH + (g + 1) * HEAD_DIM], gk, cos, sin)
             for g in range(N_KV_HEADS)], axis=0)
        ktb = kt.astype(BF16)
        sub = ktb.shape[1]
        ones_col = (lax.broadcasted_iota(jnp.int32, (sub, K_AUG - KV_WIDTH), 1) == 0).astype(BF16)
        k_ref[0, r, :] = jnp.concatenate([kt.T.astype(BF16), ones_col], axis=1)
        kf = ktb.astype(F32)
        ksq = [jnp.sum(kf[g * HEAD_DIM:(g + 1) * HEAD_DIM] ** 2, axis=0, keepdims=True) for g in range(N_KV_HEADS)]
        ksq_ref[0, :, r] = jnp.concatenate(ksq + [jnp.zeros((KSQ_ROWS - N_KV_HEADS, sub), F32)], axis=0)
        ones = jnp.ones((ONES_ROWS, qkvt.shape[1]), BF16)
        for g in range(N_KV_HEADS):
            v0 = ATTN_WIDTH + KV_WIDTH + g * HEAD_DIM
            vt_ref[0, g * V_ROWS:(g + 1) * V_ROWS, r] = jnp.concatenate(
                [qkvt[v0:v0 + HEAD_DIM].astype(BF16), ones], axis=0)
        z_ref[0, r, :] = zg[:, :POOL_WIDTH]
        sg_ref[0, r, :] = _sigmoid(zg[:, POOL_WIDTH:]).astype(BF16)


def _run_units(units, scores, accumulate):
    pending = [scores(u) for u in units[:ATTN_QK_LOOKAHEAD]]
    for i, unit in enumerate(units):
        if i + ATTN_QK_LOOKAHEAD < len(units):
            pending.append(scores(units[i + ATTN_QK_LOOKAHEAD]))
        accumulate(unit, pending.pop(0))


def _attn_kernel(kmax_ref, qt_ref, k_ref, vt_ref, o_ref, m_scr, acc_scr, qa_scr, fixed_scr):
    b = pl.program_id(0)
    ki = pl.program_id(2)
    tq = qt_ref.shape[2]
    strip = min(ATTN_STRIP, tq)
    vt = vt_ref[0]
    units = [(hd, s) for hd in range(N_HEADS) for s in range(tq // strip)]

    def pv_rows(unit):
        hd, s = unit
        return slice(hd * V_ROWS, (hd + 1) * V_ROWS), slice(s * strip, (s + 1) * strip)

    def v_aug(unit):
        g = unit[0] // GROUP
        return vt[g * V_ROWS:(g + 1) * V_ROWS]

    @pl.when(ki == 0)
    def _():
        m_scr[...] = jnp.full(m_scr.shape, -jnp.inf, F32)
        acc_scr[...] = jnp.zeros(acc_scr.shape, F32)
        fixed_scr[0] = 0

    fixed = fixed_scr[0] == 1

    @pl.when(jnp.logical_not(fixed))
    def _online():
        k = k_ref[0, :, :KV_WIDTH]

        def scores(unit):
            hd, s = unit
            return _dot(k, qt_ref[0, hd * KV_WIDTH:(hd + 1) * KV_WIDTH, s * strip:(s + 1) * strip])

        def accumulate(unit, st):
            hd = unit[0]
            rows, cols = pv_rows(unit)
            m_prev = m_scr[hd:hd + 1, cols]
            m_new = jnp.maximum(m_prev, jnp.max(st, axis=0, keepdims=True)).astype(BF16).astype(F32)
            alpha = jnp.exp2(m_prev - m_new)
            pt = jnp.exp2(st - m_new).astype(BF16)
            acc_scr[rows, cols] = alpha * acc_scr[rows, cols] + _dot(v_aug(unit), pt)
            m_scr[hd:hd + 1, cols] = m_new

        _run_units(units, scores, accumulate)

    @pl.when(fixed)
    def _fixed_reference():
        k = k_ref[0]

        def scores(unit):
            hd, s = unit
            return _dot(k, qa_scr[hd, :, s * strip:(s + 1) * strip])

        def accumulate(unit, st):
            rows, cols = pv_rows(unit)
            acc_scr[rows, cols] = acc_scr[rows, cols] + _dot(v_aug(unit), jnp.exp2(st).astype(BF16))

        _run_units(units, scores, accumulate)

    @pl.when(ki == 0)
    def _prepare_fixed_reference():
        first_row = lax.broadcasted_iota(jnp.int32, (REF_ROWS, tq), 0) == 0
        safe = None
        for hd in range(N_HEADS):
            slab = qt_ref[0, hd * KV_WIDTH:(hd + 1) * KV_WIDTH, :]
            m = m_scr[hd:hd + 1, :]
            qa_scr[hd, :KV_WIDTH, :] = slab
            qa_scr[hd, KV_WIDTH:KV_WIDTH + REF_ROWS, :] = jnp.where(first_row, -m, 0.0).astype(BF16)
            qa_scr[hd, KV_WIDTH + REF_ROWS:, :] = jnp.zeros((K_AUG - KV_WIDTH - REF_ROWS, tq), BF16)
            sf = slab.astype(F32)
            q_norm = jnp.sqrt(jnp.sum(sf * sf, axis=0, keepdims=True))
            ok = q_norm * kmax_ref[b * N_KV_HEADS + hd // GROUP] <= m + ATTN_FIXED_REF_MARGIN
            safe = ok if safe is None else jnp.logical_and(safe, ok)
        fixed_scr[0] = (jnp.min(safe.astype(F32)) > 0.5).astype(jnp.int32)

    @pl.when(ki == pl.num_programs(2) - 1)
    def _():
        ot = jnp.concatenate(
            [acc_scr[hd * V_ROWS:hd * V_ROWS + HEAD_DIM, :] / acc_scr[hd * V_ROWS + HEAD_DIM:hd * V_ROWS + HEAD_DIM + 1, :]
             for hd in range(N_HEADS)], axis=0)
        o_ref[0] = ot.T.astype(BF16)


def _pooled(zext_scr, r0, n, t0, seq_len):
    t = t0 + lax.broadcasted_iota(jnp.int32, (n, POOL_GROUP_WIDTH), 0)
    ext = n + 2 * POOL_HALO

    def ahead(v, d):
        return pltpu.roll(v, ext - d, 0)

    parts = []
    for g, w in enumerate(POOL_WINDOWS):
        left = w // 2
        right = w - 1 - left
        cols = slice(g * POOL_GROUP_WIDTH, (g + 1) * POOL_GROUP_WIDTH)
        zx = zext_scr[r0:r0 + ext, cols]
        fwd = zx
        span = 1
        while span < left:
            fwd = fwd + ahead(fwd, span)
            span *= 2
        total = (ahead(fwd, POOL_HALO - left) + ahead(fwd, POOL_HALO))[:n] if left < POOL_HALO \
            else (fwd + ahead(fwd, POOL_HALO))[:n]
        cnt = (jnp.minimum(t + right, seq_len - 1) - jnp.maximum(t - left, 0) + 1).astype(F32)
        parts.append(total / cnt - zx[POOL_HALO:POOL_HALO + n])
    return jnp.concatenate(parts, axis=1)


def _mix_ffn2_ple_kernel(h_ref, o_ref, z_ref, zprev_ref, znext_ref, sg_ref, p_ref,
                         wua_ref, wpool_ref, pscale_ref, wup_ref, wout_ref,
                         n2_ref, wg_ref, wu_ref, wd_ref,
                         npg_ref, wpg_ref, wpp_ref, npp_ref,
                         y_ref, zext_scr, a_scr, *, seq_len):
    i = pl.program_id(1)
    tm = h_ref.shape[1]
    rows = _row_slices(tm)

    zext_scr[:POOL_HALO, :] = jnp.where(i > 0, zprev_ref[0], 0.0)
    zext_scr[POOL_HALO:POOL_HALO + tm, :] = z_ref[0]
    zext_scr[POOL_HALO + tm:, :] = jnp.where(i < pl.num_programs(1) - 1, znext_ref[0], 0.0)

    avals = [_dot(o_ref[0, r, :], wua_ref[...]) for r in rows]
    eraw = [_dot(p_ref[0, r, :].astype(BF16), wpp_ref[...]) for r in rows]
    hs = []
    for r, a in zip(rows, avals):
        pooled = _pooled(zext_scr, r.start, r.stop - r.start, i * tm + r.start, seq_len).astype(BF16)
        mixed = _dot(pooled, wpool_ref[...]) * pscale_ref[...]
        b = _dot(mixed.astype(BF16), wup_ref[...])
        sg = sg_ref[0, r, :]
        m = sg[:, :D_MODEL].astype(F32) * a + sg[:, D_MODEL:].astype(F32) * b
        hs.append(h_ref[0, r, :] + _dot(m.astype(BF16), wout_ref[...]))

    xns = [_rmsnorm(h, n2_ref[...]).astype(BF16) for h in hs]
    ys = _swiglu(xns, rows, wg_ref, wu_ref, wd_ref, a_scr)
    hs = [h + 0.5 * y for h, y in zip(hs, ys)]

    es = [_rmsnorm(e, npp_ref[...]) for e in eraw]
    for r, h, e in zip(rows, hs, es):
        gate = _sigmoid(_dot(_rmsnorm(h, npg_ref[...]).astype(BF16), wpg_ref[...]))
        y_ref[0, r, :] = h + gate * e


def _tile(n, candidates=(512, 256, 128)):
    for c in candidates:
        if n % c == 0:
            return c
    raise ValueError(f"sequence length {n} must be a multiple of {candidates[-1]}")


def _resident(shape):
    return pl.BlockSpec(shape, lambda *_: (0,) * len(shape), pipeline_mode=pl.Buffered(1))


def _rope_tables(seq_len):
    rows = seq_len // GRID_W
    row = jnp.repeat(jnp.arange(rows, dtype=F32), GRID_W)
    col = jnp.tile(jnp.arange(GRID_W, dtype=F32), rows)
    half = HEAD_DIM // 2
    inv = ROPE_THETA ** (-jnp.arange(0, half, 2, dtype=F32) / half)
    ang_r = (row[:, None] * inv).T
    ang_c = (col[:, None] * inv).T
    cos = jnp.concatenate([jnp.cos(ang_r)] * 2 + [jnp.cos(ang_c)] * 2, axis=0)
    sin = jnp.concatenate([-jnp.sin(ang_r), jnp.sin(ang_r), -jnp.sin(ang_c), jnp.sin(ang_c)], axis=0)
    return cos, sin


def _prep_layer(w):
    (ffn1_norm, ffn1_w_gate, ffn1_w_up, ffn1_w_down, mix_norm, w_in, q_norm, k_norm,
     w_up_attn, pool_w, pool_scale, w_up_pool, w_out, ffn2_norm, ffn2_w_gate, ffn2_w_up,
     ffn2_w_down, ple_gate_norm, w_ple_gate, w_ple_proj, ple_post_norm) = w
    row = lambda v: v.reshape(1, -1).astype(F32)
    pool_bd = jnp.zeros((POOL_WIDTH, POOL_WIDTH), F32)
    for g in range(len(POOL_WINDOWS)):
        s = slice(g * POOL_GROUP_WIDTH, (g + 1) * POOL_GROUP_WIDTH)
        pool_bd = pool_bd.at[s, s].set(pool_w[g])
    return dict(
        n1=row(ffn1_norm), wg1=ffn1_w_gate.astype(BF16), wu1=ffn1_w_up.astype(BF16),
        wd1=ffn1_w_down.astype(BF16), nm=row(mix_norm),
        wqkvt=w_in[:, :QKV_WIDTH].T.astype(BF16), wzg=w_in[:, QKV_WIDTH:].astype(BF16),
        q_norm=q_norm.astype(F32), k_norm=k_norm.astype(F32),
        wua=w_up_attn.astype(BF16), wpool=pool_bd.astype(BF16), pscale=row(pool_scale),
        wup=w_up_pool.astype(BF16), wout=w_out.astype(BF16),
        n2=row(ffn2_norm), wg2=ffn2_w_gate.astype(BF16), wu2=ffn2_w_up.astype(BF16),
        wd2=ffn2_w_down.astype(BF16),
        npg=row(ple_gate_norm), wpg=w_ple_gate.astype(BF16), wpp=w_ple_proj.astype(BF16),
        npp=row(ple_post_norm))


def _ffn1_proj(x, lw, cos, sin):
    B, T, D = x.shape
    tm = _tile(T)
    nt = T // tm
    gq = jnp.broadcast_to(lw["q_norm"][:, None], (HEAD_DIM, tm))
    gk = jnp.broadcast_to(lw["k_norm"][:, None], (HEAD_DIM, tm))
    tile = lambda w: pl.BlockSpec((1, tm, w), lambda b, i: (b, i, 0))
    tile_t = lambda r: pl.BlockSpec((1, r, tm), lambda b, i: (b, 0, i))
    table = pl.BlockSpec((HEAD_DIM, tm), lambda b, i: (0, i))
    return pl.pallas_call(
        _ffn1_proj_kernel,
        grid=(B, nt),
        in_specs=[tile(D), _resident((1, D)), _resident((D, D_FF)), _resident((D, D_FF)),
                  _resident((D_FF, D)), _resident((1, D)), _resident((QKV_WIDTH, D)),
                  _resident((D, POOL_WIDTH + 2 * D)), _resident((HEAD_DIM, tm)),
                  _resident((HEAD_DIM, tm)), table, table],
        out_specs=[tile(D), tile_t(N_HEADS * KV_WIDTH), tile(K_AUG), tile_t(N_KV_HEADS * V_ROWS),
                   tile(POOL_WIDTH), tile(2 * D), tile_t(KSQ_ROWS)],
        out_shape=[jax.ShapeDtypeStruct((B, T, D), F32),
                   jax.ShapeDtypeStruct((B, N_HEADS * KV_WIDTH, T), BF16),
                   jax.ShapeDtypeStruct((B, T, K_AUG), BF16),
                   jax.ShapeDtypeStruct((B, N_KV_HEADS * V_ROWS, T), BF16),
                   jax.ShapeDtypeStruct((B, T, POOL_WIDTH), F32),
                   jax.ShapeDtypeStruct((B, T, 2 * D), BF16),
                   jax.ShapeDtypeStruct((B, KSQ_ROWS, T), F32)],
        scratch_shapes=[pltpu.VMEM((tm, D_FF), BF16)],
        compiler_params=pltpu.CompilerParams(
            dimension_semantics=("parallel", "parallel"), vmem_limit_bytes=VMEM_LIMIT_BYTES),
        name="ffn1_proj",
    )(x, lw["n1"], lw["wg1"], lw["wu1"], lw["wd1"], lw["nm"], lw["wqkvt"], lw["wzg"],
      gq, gk, cos, sin)


def _attention(qt, k, vt, ksq):
    B, T, _ = k.shape
    tq = _tile(T, ATTN_TILES)
    tk = _tile(T, ATTN_TILES)
    kmax = (jnp.sqrt(jnp.max(ksq[:, :N_KV_HEADS, :], axis=-1)) * 1.001).reshape(B * N_KV_HEADS)
    return pl.pallas_call(
        _attn_kernel,
        grid_spec=pltpu.PrefetchScalarGridSpec(
            num_scalar_prefetch=1,
            grid=(B, T // tq, T // tk),
            in_specs=[pl.BlockSpec((1, N_HEADS * KV_WIDTH, tq), lambda b, qi, ki, kmax: (b, 0, qi)),
                      pl.BlockSpec((1, tk, K_AUG), lambda b, qi, ki, kmax: (b, ki, 0)),
                      pl.BlockSpec((1, N_KV_HEADS * V_ROWS, tk), lambda b, qi, ki, kmax: (b, 0, ki))],
            out_specs=pl.BlockSpec((1, tq, ATTN_WIDTH), lambda b, qi, ki, kmax: (b, qi, 0)),
            scratch_shapes=[pltpu.VMEM((N_HEADS, tq), F32), pltpu.VMEM((N_HEADS * V_ROWS, tq), F32),
                            pltpu.VMEM((N_HEADS, K_AUG, tq), BF16), pltpu.SMEM((1,), jnp.int32)]),
        out_shape=jax.ShapeDtypeStruct((B, T, ATTN_WIDTH), BF16),
        compiler_params=pltpu.CompilerParams(
            dimension_semantics=("parallel", "parallel", "arbitrary"),
            vmem_limit_bytes=VMEM_LIMIT_BYTES),
        name="attention",
    )(kmax, qt, k, vt)


def _mix_ffn2_ple(h, o, z, sg, p, lw):
    B, T, D = h.shape
    tm = _tile(T)
    nt = T // tm
    halo_blocks = tm // POOL_HALO
    tile = lambda w: pl.BlockSpec((1, tm, w), lambda b, i: (b, i, 0))
    prev = pl.BlockSpec((1, POOL_HALO, POOL_WIDTH),
                        lambda b, i: (b, jnp.maximum(i * halo_blocks - 1, 0), 0))
    nxt = pl.BlockSpec((1, POOL_HALO, POOL_WIDTH),
                       lambda b, i: (b, jnp.minimum((i + 1) * halo_blocks, T // POOL_HALO - 1), 0))
    return pl.pallas_call(
        functools.partial(_mix_ffn2_ple_kernel, seq_len=T),
        grid=(B, nt),
        in_specs=[tile(D), tile(ATTN_WIDTH), tile(POOL_WIDTH), prev, nxt, tile(2 * D), tile(PLE_DIM),
                  _resident((ATTN_WIDTH, D)), _resident((POOL_WIDTH, POOL_WIDTH)),
                  _resident((1, POOL_WIDTH)), _resident((POOL_WIDTH, D)), _resident((D, D)),
                  _resident((1, D)), _resident((D, D_FF)), _resident((D, D_FF)), _resident((D_FF, D)),
                  _resident((1, D)), _resident((D, D)), _resident((PLE_DIM, D)), _resident((1, D))],
        out_specs=tile(D),
        out_shape=jax.ShapeDtypeStruct((B, T, D), F32),
        scratch_shapes=[pltpu.VMEM((tm + 2 * POOL_HALO, POOL_WIDTH), F32),
                        pltpu.VMEM((tm, D_FF), BF16)],
        compiler_params=pltpu.CompilerParams(
            dimension_semantics=("parallel", "parallel"), vmem_limit_bytes=VMEM_LIMIT_BYTES),
        name="mix_ffn2_ple",
    )(h, o, z, z, z, sg, p,
      lw["wua"], lw["wpool"], lw["pscale"], lw["wup"], lw["wout"],
      lw["n2"], lw["wg2"], lw["wu2"], lw["wd2"],
      lw["npg"], lw["wpg"], lw["wpp"], lw["npp"])


def _encoder_layer(x, p, lw):
    cos, sin = _rope_tables(x.shape[1])
    h, qt, k, vt, z, sg, ksq = _ffn1_proj(x, lw, cos, sin)
    o = _attention(qt, k, vt, ksq)
    return _mix_ffn2_ple(h, o, z, sg, p, lw)


def kernel(x_prompt, x_sample, p_prompt, p_sample, ffn1_norm, ffn1_w_gate, ffn1_w_up, ffn1_w_down,
           mix_norm, w_in, q_norm, k_norm, w_up_attn, pool_w, pool_scale, w_up_pool, w_out,
           ffn2_norm, ffn2_w_gate, ffn2_w_up, ffn2_w_down, ple_gate_norm, w_ple_gate, w_ple_proj,
           ple_post_norm):
    weights = (ffn1_norm, ffn1_w_gate, ffn1_w_up, ffn1_w_down, mix_norm, w_in, q_norm, k_norm,
               w_up_attn, pool_w, pool_scale, w_up_pool, w_out, ffn2_norm, ffn2_w_gate, ffn2_w_up,
               ffn2_w_down, ple_gate_norm, w_ple_gate, w_ple_proj, ple_post_norm)
    depth = ffn1_norm.shape[0]
    layers = [_prep_layer([w[i] for w in weights]) for i in range(depth)]

    def trunk(x, p):
        h = x
        for i in range(depth):
            h = _encoder_layer(h, p[i], layers[i])
        return h

    return (trunk(x_prompt, p_prompt), trunk(x_sample, p_sample))
```

```python
import functools

import jax
import jax.numpy as jnp
from jax import lax
from jax.experimental import pallas as pl
from jax.experimental.pallas import tpu as pltpu

F32 = jnp.float32
BF16 = jnp.bfloat16

D_MODEL = 1024
GRID_W = 64
N_HEADS = 8
N_KV_HEADS = 2
GROUP = N_HEADS // N_KV_HEADS
HEAD_DIM = 64
ATTN_WIDTH = N_HEADS * HEAD_DIM
KV_WIDTH = N_KV_HEADS * HEAD_DIM
QKV_WIDTH = ATTN_WIDTH + 2 * KV_WIDTH
POOL_WINDOWS = (2, 4, 8, 16)
POOL_WIDTH = D_MODEL // 2
POOL_GROUP_WIDTH = POOL_WIDTH // len(POOL_WINDOWS)
POOL_HALO = 8
D_FF = 2816
FF_CHUNK = 256
N_SUB = 2
PLE_DIM = 256
ROPE_THETA = 10000.0
EPS = 1e-6
SM_SCALE = HEAD_DIM ** -0.5
LOG2E = 1.4426950408889634
Q_SCALE = SM_SCALE * LOG2E
ONES_ROWS = 16
V_ROWS = HEAD_DIM + ONES_ROWS
K_AUG = 256
REF_ROWS = 16
KSQ_ROWS = 8
ATTN_FIXED_REF_MARGIN = 60.0
ATTN_TILES = (1024, 512, 256, 128)
ONLINE_STRIP, ONLINE_LOOKAHEAD = 512, 2
FIXED_STRIP, FIXED_LOOKAHEAD = 256, 1

V7X_LANES = 128
VMEM_LIMIT_BYTES = 60 * 1024 * 1024

NT_DIMS = (((1,), (1,)), ((), ()))


def _dot(a, b):
    return jnp.dot(a, b, preferred_element_type=F32)


def _sigmoid(x):
    return 1.0 / (1.0 + jnp.exp(-x))


def _rmsnorm(x, g):
    ms = jnp.mean(x * x, axis=-1, keepdims=True)
    return x * lax.rsqrt(ms + EPS) * g


def _row_slices(tm):
    sub = tm // N_SUB
    return [slice(s * sub, (s + 1) * sub) for s in range(N_SUB)]


def _swiglu(xns, rows, wg_ref, wu_ref, wd_ref, a_scr):
    for c0 in range(0, D_FF, FF_CHUNK):
        cols = slice(c0, min(c0 + FF_CHUNK, D_FF))
        for xn, r in zip(xns, rows):
            g = _dot(xn, wg_ref[:, cols])
            u = _dot(xn, wu_ref[:, cols])
            a_scr[r, cols] = (g * _sigmoid(g) * u).astype(BF16)
    return [_dot(a_scr[r, :], wd_ref[...]) for r in rows]


def _head_norm_rope_t(blk, gain, cos, sin):
    ms = jnp.mean(blk * blk, axis=0, keepdims=True)
    y = blk * lax.rsqrt(ms + EPS) * gain
    q4 = HEAD_DIM // 4
    partner = jnp.concatenate([y[q4:2 * q4], y[:q4], y[3 * q4:], y[2 * q4:3 * q4]], axis=0)
    return y * cos + partner * sin


def _ffn1_proj_kernel(x_ref, n1_ref, wg_ref, wu_ref, wd_ref, nm_ref, wqkvt_ref, wzg_ref,
                      gq_ref, gk_ref, cos_ref, sin_ref,
                      h_ref, qt_ref, k_ref, vt_ref, z_ref, sg_ref, ksq_ref, a_scr):
    rows = _row_slices(x_ref.shape[1])
    xs = [x_ref[0, r, :] for r in rows]
    xns = [_rmsnorm(x, n1_ref[...]).astype(BF16) for x in xs]
    ys = _swiglu(xns, rows, wg_ref, wu_ref, wd_ref, a_scr)
    hs = [x + 0.5 * y for x, y in zip(xs, ys)]
    for r, h in zip(rows, hs):
        h_ref[0, r, :] = h

    uns = [_rmsnorm(h, nm_ref[...]).astype(BF16) for h in hs]
    qkvts = [lax.dot_general(wqkvt_ref[...], un, NT_DIMS, preferred_element_type=F32) for un in uns]
    zgs = [_dot(un, wzg_ref[...]) for un in uns]
    for r, qkvt, zg in zip(rows, qkvts, zgs):
        cos = cos_ref[:, r]
        sin = sin_ref[:, r]
        gq = gq_ref[:, r]
        gk = gk_ref[:, r]
        for hd in range(N_HEADS):
            blk = qkvt[hd * HEAD_DIM:(hd + 1) * HEAD_DIM]
            qh = (_head_norm_rope_t(blk, gq, cos, sin) * Q_SCALE).astype(BF16)
            slabs = [jnp.zeros_like(qh)] * N_KV_HEADS
            slabs[hd // GROUP] = qh
            qt_ref[0, hd * KV_WIDTH:(hd + 1) * KV_WIDTH, r] = jnp.concatenate(slabs, axis=0)
        kt = jnp.concatenate(
            [_head_norm_rope_t(qkvt[ATTN_WIDTH + g * HEAD_DIM:ATTN_WIDTH + (g + 1) * HEAD_DIM], gk, cos, sin)
             for g in range(N_KV_HEADS)], axis=0)
        ktb = kt.astype(BF16)
        sub = ktb.shape[1]
        ones_col = (lax.broadcasted_iota(jnp.int32, (sub, K_AUG - KV_WIDTH), 1) == 0).astype(BF16)
        k_ref[0, r, :] = jnp.concatenate([kt.T.astype(BF16), ones_col], axis=1)
        kf = ktb.astype(F32)
        ksq = [jnp.sum(kf[g * HEAD_DIM:(g + 1) * HEAD_DIM] ** 2, axis=0, keepdims=True) for g in range(N_KV_HEADS)]
        ksq_ref[0, :, r] = jnp.concatenate(ksq + [jnp.zeros((KSQ_ROWS - N_KV_HEADS, sub), F32)], axis=0)
        ones = jnp.ones((ONES_ROWS, qkvt.shape[1]), BF16)
        for g in range(N_KV_HEADS):
            v0 = ATTN_WIDTH + KV_WIDTH + g * HEAD_DIM
            vt_ref[0, g * V_ROWS:(g + 1) * V_ROWS, r] = jnp.concatenate(
                [qkvt[v0:v0 + HEAD_DIM].astype(BF16), ones], axis=0)
        z_ref[0, r, :] = zg[:, :POOL_WIDTH]
        sg_ref[0, r, :] = _sigmoid(zg[:, POOL_WIDTH:]).astype(BF16)


def _run_units(tq, strip, lookahead, scores, accumulate):
    strip = min(strip, tq)
    units = [(hd, slice(s * strip, (s + 1) * strip)) for hd in range(N_HEADS) for s in range(tq // strip)]
    pending = [scores(*u) for u in units[:lookahead]]
    for i, unit in enumerate(units):
        if i + lookahead < len(units):
            pending.append(scores(*units[i + lookahead]))
        accumulate(*unit, pending.pop(0))


def _attn_kernel(kmax_ref, qt_ref, k_ref, vt_ref, o_ref, m_scr, acc_scr, qa_scr, fixed_scr):
    b = pl.program_id(0)
    ki = pl.program_id(2)
    tq = qt_ref.shape[2]
    vt = vt_ref[0]

    def acc_rows(hd):
        return slice(hd * V_ROWS, (hd + 1) * V_ROWS)

    def v_aug(hd):
        g = hd // GROUP
        return vt[g * V_ROWS:(g + 1) * V_ROWS]

    @pl.when(ki == 0)
    def _():
        m_scr[...] = jnp.full(m_scr.shape, -jnp.inf, F32)
        acc_scr[...] = jnp.zeros(acc_scr.shape, F32)
        fixed_scr[0] = 0

    fixed = fixed_scr[0] == 1

    @pl.when(jnp.logical_not(fixed))
    def _online():
        k = k_ref[0, :, :KV_WIDTH]

        def scores(hd, cols):
            return _dot(k, qt_ref[0, hd * KV_WIDTH:(hd + 1) * KV_WIDTH, cols])

        def accumulate(hd, cols, st):
            rows = acc_rows(hd)
            m_prev = m_scr[hd:hd + 1, cols]
            m_new = jnp.maximum(m_prev, jnp.max(st, axis=0, keepdims=True)).astype(BF16).astype(F32)
            alpha = jnp.exp2(m_prev - m_new)
            pt = jnp.exp2(st - m_new).astype(BF16)
            acc_scr[rows, cols] = alpha * acc_scr[rows, cols] + _dot(v_aug(hd), pt)
            m_scr[hd:hd + 1, cols] = m_new

        _run_units(tq, ONLINE_STRIP, ONLINE_LOOKAHEAD, scores, accumulate)

    @pl.when(fixed)
    def _fixed_reference():
        k = k_ref[0]

        def scores(hd, cols):
            return _dot(k, qa_scr[hd, :, cols])

        def accumulate(hd, cols, st):
            rows = acc_rows(hd)
            acc_scr[rows, cols] = acc_scr[rows, cols] + _dot(v_aug(hd), jnp.exp2(st).astype(BF16))

        _run_units(tq, FIXED_STRIP, FIXED_LOOKAHEAD, scores, accumulate)

    @pl.when(ki == 0)
    def _prepare_fixed_reference():
        first_row = lax.broadcasted_iota(jnp.int32, (REF_ROWS, tq), 0) == 0
        safe = None
        for hd in range(N_HEADS):
            slab = qt_ref[0, hd * KV_WIDTH:(hd + 1) * KV_WIDTH, :]
            m = m_scr[hd:hd + 1, :]
            qa_scr[hd, :KV_WIDTH, :] = slab
            qa_scr[hd, KV_WIDTH:KV_WIDTH + REF_ROWS, :] = jnp.where(first_row, -m, 0.0).astype(BF16)
            qa_scr[hd, KV_WIDTH + REF_ROWS:, :] = jnp.zeros((K_AUG - KV_WIDTH - REF_ROWS, tq), BF16)
            sf = slab.astype(F32)
            q_norm = jnp.sqrt(jnp.sum(sf * sf, axis=0, keepdims=True))
            ok = q_norm * kmax_ref[b * N_KV_HEADS + hd // GROUP] <= m + ATTN_FIXED_REF_MARGIN
            safe = ok if safe is None else jnp.logical_and(safe, ok)
        fixed_scr[0] = (jnp.min(safe.astype(F32)) > 0.5).astype(jnp.int32)

    @pl.when(ki == pl.num_programs(2) - 1)
    def _():
        ot = jnp.concatenate(
            [acc_scr[hd * V_ROWS:hd * V_ROWS + HEAD_DIM, :] / acc_scr[hd * V_ROWS + HEAD_DIM:hd * V_ROWS + HEAD_DIM + 1, :]
             for hd in range(N_HEADS)], axis=0)
        o_ref[0] = ot.T.astype(BF16)


def _pooled(zext_scr, r0, n, t0, seq_len):
    t = t0 + lax.broadcasted_iota(jnp.int32, (n, POOL_GROUP_WIDTH), 0)
    ext = n + 2 * POOL_HALO

    def ahead(v, d):
        return pltpu.roll(v, ext - d, 0)

    parts = []
    for g, w in enumerate(POOL_WINDOWS):
        left = w // 2
        right = w - 1 - left
        cols = slice(g * POOL_GROUP_WIDTH, (g + 1) * POOL_GROUP_WIDTH)
        zx = zext_scr[r0:r0 + ext, cols]
        fwd = zx
        span = 1
        while span < left:
            fwd = fwd + ahead(fwd, span)
            span *= 2
        total = (ahead(fwd, POOL_HALO - left) + ahead(fwd, POOL_HALO))[:n] if left < POOL_HALO \
            else (fwd + ahead(fwd, POOL_HALO))[:n]
        cnt = (jnp.minimum(t + right, seq_len - 1) - jnp.maximum(t - left, 0) + 1).astype(F32)
        parts.append(total / cnt - zx[POOL_HALO:POOL_HALO + n])
    return jnp.concatenate(parts, axis=1)


def _mix_ffn2_ple_kernel(h_ref, o_ref, z_ref, zprev_ref, znext_ref, sg_ref, p_ref,
                         wua_ref, wpool_ref, pscale_ref, wup_ref, wout_ref,
                         n2_ref, wg_ref, wu_ref, wd_ref,
                         npg_ref, wpg_ref, wpp_ref, npp_ref,
                         y_ref, zext_scr, a_scr, *, seq_len):
    i = pl.program_id(1)
    tm = h_ref.shape[1]
    rows = _row_slices(tm)

    zext_scr[:POOL_HALO, :] = jnp.where(i > 0, zprev_ref[0], 0.0)
    zext_scr[POOL_HALO:POOL_HALO + tm, :] = z_ref[0]
    zext_scr[POOL_HALO + tm:, :] = jnp.where(i < pl.num_programs(1) - 1, znext_ref[0], 0.0)

    avals = [_dot(o_ref[0, r, :], wua_ref[...]) for r in rows]
    eraw = [_dot(p_ref[0, r, :].astype(BF16), wpp_ref[...]) for r in rows]
    hs = []
    for r, a in zip(rows, avals):
        pooled = _pooled(zext_scr, r.start, r.stop - r.start, i * tm + r.start, seq_len).astype(BF16)
        mixed = _dot(pooled, wpool_ref[...]) * pscale_ref[...]
        b = _dot(mixed.astype(BF16), wup_ref[...])
        sg = sg_ref[0, r, :]
        m = sg[:, :D_MODEL].astype(F32) * a + sg[:, D_MODEL:].astype(F32) * b
        hs.append(h_ref[0, r, :] + _dot(m.astype(BF16), wout_ref[...]))

    xns = [_rmsnorm(h, n2_ref[...]).astype(BF16) for h in hs]
    ys = _swiglu(xns, rows, wg_ref, wu_ref, wd_ref, a_scr)
    hs = [h + 0.5 * y for h, y in zip(hs, ys)]

    es = [_rmsnorm(e, npp_ref[...]) for e in eraw]
    for r, h, e in zip(rows, hs, es):
        gate = _sigmoid(_dot(_rmsnorm(h, npg_ref[...]).astype(BF16), wpg_ref[...]))
        y_ref[0, r, :] = h + gate * e


def _tile(n, candidates=(512, 256, 128)):
    for c in candidates:
        if n % c == 0:
            return c
    raise ValueError(f"sequence length {n} must be a multiple of {candidates[-1]}")


def _resident(shape):
    return pl.BlockSpec(shape, lambda *_: (0,) * len(shape), pipeline_mode=pl.Buffered(1))


def _rope_tables(seq_len):
    rows = seq_len // GRID_W
    row = jnp.repeat(jnp.arange(rows, dtype=F32), GRID_W)
    col = jnp.tile(jnp.arange(GRID_W, dtype=F32), rows)
    half = HEAD_DIM // 2
    inv = ROPE_THETA ** (-jnp.arange(0, half, 2, dtype=F32) / half)
    ang_r = (row[:, None] * inv).T
    ang_c = (col[:, None] * inv).T
    cos = jnp.concatenate([jnp.cos(ang_r)] * 2 + [jnp.cos(ang_c)] * 2, axis=0)
    sin = jnp.concatenate([-jnp.sin(ang_r), jnp.sin(ang_r), -jnp.sin(ang_c), jnp.sin(ang_c)], axis=0)
    return cos, sin


def _prep_layer(w):
    (ffn1_norm, ffn1_w_gate, ffn1_w_up, ffn1_w_down, mix_norm, w_in, q_norm, k_norm,
     w_up_attn, pool_w, pool_scale, w_up_pool, w_out, ffn2_norm, ffn2_w_gate, ffn2_w_up,
     ffn2_w_down, ple_gate_norm, w_ple_gate, w_ple_proj, ple_post_norm) = w
    row = lambda v: v.reshape(1, -1).astype(F32)
    pool_bd = jnp.zeros((POOL_WIDTH, POOL_WIDTH), F32)
    for g in range(len(POOL_WINDOWS)):
        s = slice(g * POOL_GROUP_WIDTH, (g + 1) * POOL_GROUP_WIDTH)
        pool_bd = pool_bd.at[s, s].set(pool_w[g])
    return dict(
        n1=row(ffn1_norm), wg1=ffn1_w_gate.astype(BF16), wu1=ffn1_w_up.astype(BF16),
        wd1=ffn1_w_down.astype(BF16), nm=row(mix_norm),
        wqkvt=w_in[:, :QKV_WIDTH].T.astype(BF16), wzg=w_in[:, QKV_WIDTH:].astype(BF16),
        q_norm=q_norm.astype(F32), k_norm=k_norm.astype(F32),
        wua=w_up_attn.astype(BF16), wpool=pool_bd.astype(BF16), pscale=row(pool_scale),
        wup=w_up_pool.astype(BF16), wout=w_out.astype(BF16),
        n2=row(ffn2_norm), wg2=ffn2_w_gate.astype(BF16), wu2=ffn2_w_up.astype(BF16),
        wd2=ffn2_w_down.astype(BF16),
        npg=row(ple_gate_norm), wpg=w_ple_gate.astype(BF16), wpp=w_ple_proj.astype(BF16),
        npp=row(ple_post_norm))


def _ffn1_proj(x, lw, cos, sin):
    B, T, D = x.shape
    tm = _tile(T)
    nt = T // tm
    gq = jnp.broadcast_to(lw["q_norm"][:, None], (HEAD_DIM, tm))
    gk = jnp.broadcast_to(lw["k_norm"][:, None], (HEAD_DIM, tm))
    tile = lambda w: pl.BlockSpec((1, tm, w), lambda b, i: (b, i, 0))
    tile_t = lambda r: pl.BlockSpec((1, r, tm), lambda b, i: (b, 0, i))
    table = pl.BlockSpec((HEAD_DIM, tm), lambda b, i: (0, i))
    return pl.pallas_call(
        _ffn1_proj_kernel,
        grid=(B, nt),
        in_specs=[tile(D), _resident((1, D)), _resident((D, D_FF)), _resident((D, D_FF)),
                  _resident((D_FF, D)), _resident((1, D)), _resident((QKV_WIDTH, D)),
                  _resident((D, POOL_WIDTH + 2 * D)), _resident((HEAD_DIM, tm)),
                  _resident((HEAD_DIM, tm)), table, table],
        out_specs=[tile(D), tile_t(N_HEADS * KV_WIDTH), tile(K_AUG), tile_t(N_KV_HEADS * V_ROWS),
                   tile(POOL_WIDTH), tile(2 * D), tile_t(KSQ_ROWS)],
        out_shape=[jax.ShapeDtypeStruct((B, T, D), F32),
                   jax.ShapeDtypeStruct((B, N_HEADS * KV_WIDTH, T), BF16),
                   jax.ShapeDtypeStruct((B, T, K_AUG), BF16),
                   jax.ShapeDtypeStruct((B, N_KV_HEADS * V_ROWS, T), BF16),
                   jax.ShapeDtypeStruct((B, T, POOL_WIDTH), F32),
                   jax.ShapeDtypeStruct((B, T, 2 * D), BF16),
                   jax.ShapeDtypeStruct((B, KSQ_ROWS, T), F32)],
        scratch_shapes=[pltpu.VMEM((tm, D_FF), BF16)],
        compiler_params=pltpu.CompilerParams(
            dimension_semantics=("parallel", "parallel"), vmem_limit_bytes=VMEM_LIMIT_BYTES),
        name="ffn1_proj",
    )(x, lw["n1"], lw["wg1"], lw["wu1"], lw["wd1"], lw["nm"], lw["wqkvt"], lw["wzg"],
      gq, gk, cos, sin)


def _attention(qt, k, vt, ksq):
    B, T, _ = k.shape
    tq = _tile(T, ATTN_TILES)
    tk = _tile(T, ATTN_TILES)
    kmax = (jnp.sqrt(jnp.max(ksq[:, :N_KV_HEADS, :], axis=-1)) * 1.001).reshape(B * N_KV_HEADS)
    return pl.pallas_call(
        _attn_kernel,
        grid_spec=pltpu.PrefetchScalarGridSpec(
            num_scalar_prefetch=1,
            grid=(B, T // tq, T // tk),
            in_specs=[pl.BlockSpec((1, N_HEADS * KV_WIDTH, tq), lambda b, qi, ki, kmax: (b, 0, qi)),
                      pl.BlockSpec((1, tk, K_AUG), lambda b, qi, ki, kmax: (b, ki, 0)),
                      pl.BlockSpec((1, N_KV_HEADS * V_ROWS, tk), lambda b, qi, ki, kmax: (b, 0, ki))],
            out_specs=pl.BlockSpec((1, tq, ATTN_WIDTH), lambda b, qi, ki, kmax: (b, qi, 0)),
            scratch_shapes=[pltpu.VMEM((N_HEADS, tq), F32), pltpu.VMEM((N_HEADS * V_ROWS, tq), F32),
                            pltpu.VMEM((N_HEADS, K_AUG, tq), BF16), pltpu.SMEM((1,), jnp.int32)]),
        out_shape=jax.ShapeDtypeStruct((B, T, ATTN_WIDTH), BF16),
        compiler_params=pltpu.CompilerParams(
            dimension_semantics=("parallel", "parallel", "arbitrary"),
            vmem_limit_bytes=VMEM_LIMIT_BYTES),
        name="attention",
    )(kmax, qt, k, vt)


def _mix_ffn2_ple(h, o, z, sg, p, lw):
    B, T, D = h.shape
    tm = _tile(T)
    nt = T // tm
    halo_blocks = tm // POOL_HALO
    tile = lambda w: pl.BlockSpec((1, tm, w), lambda b, i: (b, i, 0))
    prev = pl.BlockSpec((1, POOL_HALO, POOL_WIDTH),
                        lambda b, i: (b, jnp.maximum(i * halo_blocks - 1, 0), 0))
    nxt = pl.BlockSpec((1, POOL_HALO, POOL_WIDTH),
                       lambda b, i: (b, jnp.minimum((i + 1) * halo_blocks, T // POOL_HALO - 1), 0))
    return pl.pallas_call(
        functools.partial(_mix_ffn2_ple_kernel, seq_len=T),
        grid=(B, nt),
        in_specs=[tile(D), tile(ATTN_WIDTH), tile(POOL_WIDTH), prev, nxt, tile(2 * D), tile(PLE_DIM),
                  _resident((ATTN_WIDTH, D)), _resident((POOL_WIDTH, POOL_WIDTH)),
                  _resident((1, POOL_WIDTH)), _resident((POOL_WIDTH, D)), _resident((D, D)),
                  _resident((1, D)), _resident((D, D_FF)), _resident((D, D_FF)), _resident((D_FF, D)),
                  _resident((1, D)), _resident((D, D)), _resident((PLE_DIM, D)), _resident((1, D))],
        out_specs=tile(D),
        out_shape=jax.ShapeDtypeStruct((B, T, D), F32),
        scratch_shapes=[pltpu.VMEM((tm + 2 * POOL_HALO, POOL_WIDTH), F32),
                        pltpu.VMEM((tm, D_FF), BF16)],
        compiler_params=pltpu.CompilerParams(
            dimension_semantics=("parallel", "parallel"), vmem_limit_bytes=VMEM_LIMIT_BYTES),
        name="mix_ffn2_ple",
    )(h, o, z, z, z, sg, p,
      lw["wua"], lw["wpool"], lw["pscale"], lw["wup"], lw["wout"],
      lw["n2"], lw["wg2"], lw["wu2"], lw["wd2"],
      lw["npg"], lw["wpg"], lw["wpp"], lw["npp"])


def _encoder_layer(x, p, lw):
    cos, sin = _rope_tables(x.shape[1])
    h, qt, k, vt, z, sg, ksq = _ffn1_proj(x, lw, cos, sin)
    o = _attention(qt, k, vt, ksq)
    return _mix_ffn2_ple(h, o, z, sg, p, lw)


def kernel(x_prompt, x_sample, p_prompt, p_sample, ffn1_norm, ffn1_w_gate, ffn1_w_up, ffn1_w_down,
           mix_norm, w_in, q_norm, k_norm, w_up_attn, pool_w, pool_scale, w_up_pool, w_out,
           ffn2_norm, ffn2_w_gate, ffn2_w_up, ffn2_w_down, ple_gate_norm, w_ple_gate, w_ple_proj,
           ple_post_norm):
    weights = (ffn1_norm, ffn1_w_gate, ffn1_w_up, ffn1_w_down, mix_norm, w_in, q_norm, k_norm,
               w_up_attn, pool_w, pool_scale, w_up_pool, w_out, ffn2_norm, ffn2_w_gate, ffn2_w_up,
               ffn2_w_down, ple_gate_norm, w_ple_gate, w_ple_proj, ple_post_norm)
    depth = ffn1_norm.shape[0]
    layers = [_prep_layer([w[i] for w in weights]) for i in range(depth)]

    def trunk(x, p):
        h = x
        for i in range(depth):
            h = _encoder_layer(h, p[i], layers[i])
        return h

    return (trunk(x_prompt, p_prompt), trunk(x_sample, p_sample))
```

```python
import functools

import jax
import jax.numpy as jnp
from jax import lax
from jax.experimental import pallas as pl
from jax.experimental.pallas import tpu as pltpu

F32 = jnp.float32
BF16 = jnp.bfloat16

D_MODEL = 1024
GRID_W = 64
N_HEADS = 8
N_KV_HEADS = 2
GROUP = N_HEADS // N_KV_HEADS
HEAD_DIM = 64
ATTN_WIDTH = N_HEADS * HEAD_DIM
KV_WIDTH = N_KV_HEADS * HEAD_DIM
QKV_WIDTH = ATTN_WIDTH + 2 * KV_WIDTH
POOL_WINDOWS = (2, 4, 8, 16)
POOL_WIDTH = D_MODEL // 2
POOL_GROUP_WIDTH = POOL_WIDTH // len(POOL_WINDOWS)
POOL_HALO = 8
D_FF = 2816
FF_CHUNK = 256
N_SUB = 2
PLE_DIM = 256
ROPE_THETA = 10000.0
EPS = 1e-6
SM_SCALE = HEAD_DIM ** -0.5
LOG2E = 1.4426950408889634
Q_SCALE = SM_SCALE * LOG2E
ONES_ROWS = 16
V_ROWS = HEAD_DIM + ONES_ROWS
K_AUG = 256
REF_ROWS = 16
KSQ_ROWS = 8
ATTN_FIXED_REF_MARGIN = 60.0
PROBE_KEYS = 128
ATTN_Q_TILES = (1024, 512, 256, 128)
ATTN_K_TILES = (2048, 1024, 512, 256, 128)
ONLINE_STRIP, ONLINE_LOOKAHEAD = 512, 2
FIXED_STRIP, FIXED_LOOKAHEAD = 256, 2

V7X_LANES = 128
VMEM_LIMIT_BYTES = 60 * 1024 * 1024

NT_DIMS = (((1,), (1,)), ((), ()))


def _dot(a, b):
    return jnp.dot(a, b, preferred_element_type=F32)


def _sigmoid(x):
    return 1.0 / (1.0 + jnp.exp(-x))


def _rmsnorm(x, g):
    ms = jnp.mean(x * x, axis=-1, keepdims=True)
    return x * lax.rsqrt(ms + EPS) * g


def _row_slices(tm):
    sub = tm // N_SUB
    return [slice(s * sub, (s + 1) * sub) for s in range(N_SUB)]


def _swiglu(xns, rows, wg_ref, wu_ref, wd_ref, a_scr):
    for c0 in range(0, D_FF, FF_CHUNK):
        cols = slice(c0, min(c0 + FF_CHUNK, D_FF))
        for xn, r in zip(xns, rows):
            g = _dot(xn, wg_ref[:, cols])
            u = _dot(xn, wu_ref[:, cols])
            a_scr[r, cols] = (g * _sigmoid(g) * u).astype(BF16)
    return [_dot(a_scr[r, :], wd_ref[...]) for r in rows]


def _head_norm_rope_t(blk, gain, cos, sin):
    ms = jnp.mean(blk * blk, axis=0, keepdims=True)
    y = blk * lax.rsqrt(ms + EPS) * gain
    q4 = HEAD_DIM // 4
    partner = jnp.concatenate([y[q4:2 * q4], y[:q4], y[3 * q4:], y[2 * q4:3 * q4]], axis=0)
    return y * cos + partner * sin


def _ffn1_proj_kernel(x_ref, n1_ref, wg_ref, wu_ref, wd_ref, nm_ref, wqkvt_ref, wzg_ref,
                      gq_ref, gk_ref, cos_ref, sin_ref,
                      h_ref, qt_ref, k_ref, vt_ref, z_ref, sg_ref, ksq_ref, a_scr):
    rows = _row_slices(x_ref.shape[1])
    xs = [x_ref[0, r, :] for r in rows]
    xns = [_rmsnorm(x, n1_ref[...]).astype(BF16) for x in xs]
    ys = _swiglu(xns, rows, wg_ref, wu_ref, wd_ref, a_scr)
    hs = [x + 0.5 * y for x, y in zip(xs, ys)]
    for r, h in zip(rows, hs):
        h_ref[0, r, :] = h

    uns = [_rmsnorm(h, nm_ref[...]).astype(BF16) for h in hs]
    qkvts = [lax.dot_general(wqkvt_ref[...], un, NT_DIMS, preferred_element_type=F32) for un in uns]
    zgs = [_dot(un, wzg_ref[...]) for un in uns]
    for r, qkvt, zg in zip(rows, qkvts, zgs):
        cos = cos_ref[:, r]
        sin = sin_ref[:, r]
        gq = gq_ref[:, r]
        gk = gk_ref[:, r]
        for hd in range(N_HEADS):
            blk = qkvt[hd * HEAD_DIM:(hd + 1) * HEAD_DIM]
            qh = (_head_norm_rope_t(blk, gq, cos, sin) * Q_SCALE).astype(BF16)
            slabs = [jnp.zeros_like(qh)] * N_KV_HEADS
            slabs[hd // GROUP] = qh
            qt_ref[0, hd * KV_WIDTH:(hd + 1) * KV_WIDTH, r] = jnp.concatenate(slabs, axis=0)
        kt = jnp.concatenate(
            [_head_norm_rope_t(qkvt[ATTN_WIDTH + g * HEAD_DIM:ATTN_WIDTH + (g + 1) * HEAD_DIM], gk, cos, sin)
             for g in range(N_KV_HEADS)], axis=0)
        ktb = kt.astype(BF16)
        sub = ktb.shape[1]
        ones_col = (lax.broadcasted_iota(jnp.int32, (sub, K_AUG - KV_WIDTH), 1) == 0).astype(BF16)
        k_ref[0, r, :] = jnp.concatenate([kt.T.astype(BF16), ones_col], axis=1)
        kf = ktb.astype(F32)
        ksq = [jnp.sum(kf[g * HEAD_DIM:(g + 1) * HEAD_DIM] ** 2, axis=0, keepdims=True) for g in range(N_KV_HEADS)]
        ksq_ref[0, :, r] = jnp.concatenate(ksq + [jnp.zeros((KSQ_ROWS - N_KV_HEADS, sub), F32)], axis=0)
        ones = jnp.ones((ONES_ROWS, qkvt.shape[1]), BF16)
        for g in range(N_KV_HEADS):
            v0 = ATTN_WIDTH + KV_WIDTH + g * HEAD_DIM
            vt_ref[0, g * V_ROWS:(g + 1) * V_ROWS, r] = jnp.concatenate(
                [qkvt[v0:v0 + HEAD_DIM].astype(BF16), ones], axis=0)
        z_ref[0, r, :] = zg[:, :POOL_WIDTH]
        sg_ref[0, r, :] = _sigmoid(zg[:, POOL_WIDTH:]).astype(BF16)


def _run_units(tq, strip, lookahead, scores, accumulate):
    strip = min(strip, tq)
    units = [(hd, slice(s * strip, (s + 1) * strip)) for hd in range(N_HEADS) for s in range(tq // strip)]
    pending = [scores(*u) for u in units[:lookahead]]
    for i, unit in enumerate(units):
        if i + lookahead < len(units):
            pending.append(scores(*units[i + lookahead]))
        accumulate(*unit, pending.pop(0))


def _attn_kernel(kmax_ref, qt_ref, k_ref, vt_ref, o_ref, m_scr, acc_scr, qa_scr, fixed_scr):
    b = pl.program_id(0)
    ki = pl.program_id(2)
    tq = qt_ref.shape[2]
    vt = vt_ref[0]

    def acc_rows(hd):
        return slice(hd * V_ROWS, (hd + 1) * V_ROWS)

    def v_aug(hd):
        g = hd // GROUP
        return vt[g * V_ROWS:(g + 1) * V_ROWS]

    @pl.when(ki == 0)
    def _start():
        acc_scr[...] = jnp.zeros(acc_scr.shape, F32)
        k_probe = k_ref[0, :PROBE_KEYS, :KV_WIDTH]
        first_row = lax.broadcasted_iota(jnp.int32, (REF_ROWS, tq), 0) == 0
        safe = None
        for hd in range(N_HEADS):
            slab = qt_ref[0, hd * KV_WIDTH:(hd + 1) * KV_WIDTH, :]
            m = jnp.max(_dot(k_probe, slab), axis=0, keepdims=True).astype(BF16).astype(F32)
            m_scr[hd:hd + 1, :] = m
            qa_scr[hd, :KV_WIDTH, :] = slab
            qa_scr[hd, KV_WIDTH:KV_WIDTH + REF_ROWS, :] = jnp.where(first_row, -m, 0.0).astype(BF16)
            qa_scr[hd, KV_WIDTH + REF_ROWS:, :] = jnp.zeros((K_AUG - KV_WIDTH - REF_ROWS, tq), BF16)
            sf = slab.astype(F32)
            q_norm = jnp.sqrt(jnp.sum(sf * sf, axis=0, keepdims=True))
            ok = q_norm * kmax_ref[b * N_KV_HEADS + hd // GROUP] <= m + ATTN_FIXED_REF_MARGIN
            safe = ok if safe is None else jnp.logical_and(safe, ok)
        fixed_scr[0] = (jnp.min(safe.astype(F32)) > 0.5).astype(jnp.int32)

    fixed = fixed_scr[0] == 1

    @pl.when(jnp.logical_not(fixed))
    def _online():
        k = k_ref[0, :, :KV_WIDTH]

        def scores(hd, cols):
            return _dot(k, qt_ref[0, hd * KV_WIDTH:(hd + 1) * KV_WIDTH, cols])

        def accumulate(hd, cols, st):
            rows = acc_rows(hd)
            m_prev = m_scr[hd:hd + 1, cols]
            m_new = jnp.maximum(m_prev, jnp.max(st, axis=0, keepdims=True)).astype(BF16).astype(F32)
            alpha = jnp.exp2(m_prev - m_new)
            pt = jnp.exp2(st - m_new).astype(BF16)
            acc_scr[rows, cols] = alpha * acc_scr[rows, cols] + _dot(v_aug(hd), pt)
            m_scr[hd:hd + 1, cols] = m_new

        _run_units(tq, ONLINE_STRIP, ONLINE_LOOKAHEAD, scores, accumulate)

    @pl.when(fixed)
    def _fixed_reference():
        k = k_ref[0]

        def scores(hd, cols):
            return _dot(k, qa_scr[hd, :, cols])

        def accumulate(hd, cols, st):
            rows = acc_rows(hd)
            acc_scr[rows, cols] = acc_scr[rows, cols] + _dot(v_aug(hd), jnp.exp2(st).astype(BF16))

        _run_units(tq, FIXED_STRIP, FIXED_LOOKAHEAD, scores, accumulate)

    @pl.when(ki == pl.num_programs(2) - 1)
    def _():
        ot = jnp.concatenate(
            [acc_scr[hd * V_ROWS:hd * V_ROWS + HEAD_DIM, :] / acc_scr[hd * V_ROWS + HEAD_DIM:hd * V_ROWS + HEAD_DIM + 1, :]
             for hd in range(N_HEADS)], axis=0)
        o_ref[0] = ot.T.astype(BF16)


def _pooled(zext_scr, r0, n, t0, seq_len):
    t = t0 + lax.broadcasted_iota(jnp.int32, (n, POOL_GROUP_WIDTH), 0)
    ext = n + 2 * POOL_HALO

    def ahead(v, d):
        return pltpu.roll(v, ext - d, 0)

    parts = []
    for g, w in enumerate(POOL_WINDOWS):
        left = w // 2
        right = w - 1 - left
        cols = slice(g * POOL_GROUP_WIDTH, (g + 1) * POOL_GROUP_WIDTH)
        zx = zext_scr[r0:r0 + ext, cols]
        fwd = zx
        span = 1
        while span < left:
            fwd = fwd + ahead(fwd, span)
            span *= 2
        total = (ahead(fwd, POOL_HALO - left) + ahead(fwd, POOL_HALO))[:n] if left < POOL_HALO \
            else (fwd + ahead(fwd, POOL_HALO))[:n]
        cnt = (jnp.minimum(t + right, seq_len - 1) - jnp.maximum(t - left, 0) + 1).astype(F32)
        parts.append(total / cnt - zx[POOL_HALO:POOL_HALO + n])
    return jnp.concatenate(parts, axis=1)


def _mix_ffn2_ple_kernel(h_ref, o_ref, z_ref, zprev_ref, znext_ref, sg_ref, p_ref,
                         wua_ref, wpool_ref, pscale_ref, wup_ref, wout_ref,
                         n2_ref, wg_ref, wu_ref, wd_ref,
                         npg_ref, wpg_ref, wpp_ref, npp_ref,
                         y_ref, zext_scr, a_scr, *, seq_len):
    i = pl.program_id(1)
    tm = h_ref.shape[1]
    rows = _row_slices(tm)

    zext_scr[:POOL_HALO, :] = jnp.where(i > 0, zprev_ref[0], 0.0)
    zext_scr[POOL_HALO:POOL_HALO + tm, :] = z_ref[0]
    zext_scr[POOL_HALO + tm:, :] = jnp.where(i < pl.num_programs(1) - 1, znext_ref[0], 0.0)

    avals = [_dot(o_ref[0, r, :], wua_ref[...]) for r in rows]
    eraw = [_dot(p_ref[0, r, :].astype(BF16), wpp_ref[...]) for r in rows]
    hs = []
    for r, a in zip(rows, avals):
        pooled = _pooled(zext_scr, r.start, r.stop - r.start, i * tm + r.start, seq_len).astype(BF16)
        mixed = _dot(pooled, wpool_ref[...]) * pscale_ref[...]
        b = _dot(mixed.astype(BF16), wup_ref[...])
        sg = sg_ref[0, r, :]
        m = sg[:, :D_MODEL].astype(F32) * a + sg[:, D_MODEL:].astype(F32) * b
        hs.append(h_ref[0, r, :] + _dot(m.astype(BF16), wout_ref[...]))

    xns = [_rmsnorm(h, n2_ref[...]).astype(BF16) for h in hs]
    ys = _swiglu(xns, rows, wg_ref, wu_ref, wd_ref, a_scr)
    hs = [h + 0.5 * y for h, y in zip(hs, ys)]

    es = [_rmsnorm(e, npp_ref[...]) for e in eraw]
    for r, h, e in zip(rows, hs, es):
        gate = _sigmoid(_dot(_rmsnorm(h, npg_ref[...]).astype(BF16), wpg_ref[...]))
        y_ref[0, r, :] = h + gate * e


def _tile(n, candidates=(512, 256, 128)):
    for c in candidates:
        if n % c == 0:
            return c
    raise ValueError(f"sequence length {n} must be a multiple of {candidates[-1]}")


def _resident(shape):
    return pl.BlockSpec(shape, lambda *_: (0,) * len(shape), pipeline_mode=pl.Buffered(1))


def _rope_tables(seq_len):
    rows = seq_len // GRID_W
    row = jnp.repeat(jnp.arange(rows, dtype=F32), GRID_W)
    col = jnp.tile(jnp.arange(GRID_W, dtype=F32), rows)
    half = HEAD_DIM // 2
    inv = ROPE_THETA ** (-jnp.arange(0, half, 2, dtype=F32) / half)
    ang_r = (row[:, None] * inv).T
    ang_c = (col[:, None] * inv).T
    cos = jnp.concatenate([jnp.cos(ang_r)] * 2 + [jnp.cos(ang_c)] * 2, axis=0)
    sin = jnp.concatenate([-jnp.sin(ang_r), jnp.sin(ang_r), -jnp.sin(ang_c), jnp.sin(ang_c)], axis=0)
    return cos, sin


def _prep_layer(w):
    (ffn1_norm, ffn1_w_gate, ffn1_w_up, ffn1_w_down, mix_norm, w_in, q_norm, k_norm,
     w_up_attn, pool_w, pool_scale, w_up_pool, w_out, ffn2_norm, ffn2_w_gate, ffn2_w_up,
     ffn2_w_down, ple_gate_norm, w_ple_gate, w_ple_proj, ple_post_norm) = w
    row = lambda v: v.reshape(1, -1).astype(F32)
    pool_bd = jnp.zeros((POOL_WIDTH, POOL_WIDTH), F32)
    for g in range(len(POOL_WINDOWS)):
        s = slice(g * POOL_GROUP_WIDTH, (g + 1) * POOL_GROUP_WIDTH)
        pool_bd = pool_bd.at[s, s].set(pool_w[g])
    return dict(
        n1=row(ffn1_norm), wg1=ffn1_w_gate.astype(BF16), wu1=ffn1_w_up.astype(BF16),
        wd1=ffn1_w_down.astype(BF16), nm=row(mix_norm),
        wqkvt=w_in[:, :QKV_WIDTH].T.astype(BF16), wzg=w_in[:, QKV_WIDTH:].astype(BF16),
        q_norm=q_norm.astype(F32), k_norm=k_norm.astype(F32),
        wua=w_up_attn.astype(BF16), wpool=pool_bd.astype(BF16), pscale=row(pool_scale),
        wup=w_up_pool.astype(BF16), wout=w_out.astype(BF16),
        n2=row(ffn2_norm), wg2=ffn2_w_gate.astype(BF16), wu2=ffn2_w_up.astype(BF16),
        wd2=ffn2_w_down.astype(BF16),
        npg=row(ple_gate_norm), wpg=w_ple_gate.astype(BF16), wpp=w_ple_proj.astype(BF16),
        npp=row(ple_post_norm))


def _ffn1_proj(x, lw, cos, sin):
    B, T, D = x.shape
    tm = _tile(T)
    nt = T // tm
    gq = jnp.broadcast_to(lw["q_norm"][:, None], (HEAD_DIM, tm))
    gk = jnp.broadcast_to(lw["k_norm"][:, None], (HEAD_DIM, tm))
    tile = lambda w: pl.BlockSpec((1, tm, w), lambda b, i: (b, i, 0))
    tile_t = lambda r: pl.BlockSpec((1, r, tm), lambda b, i: (b, 0, i))
    table = pl.BlockSpec((HEAD_DIM, tm), lambda b, i: (0, i))
    return pl.pallas_call(
        _ffn1_proj_kernel,
        grid=(B, nt),
        in_specs=[tile(D), _resident((1, D)), _resident((D, D_FF)), _resident((D, D_FF)),
                  _resident((D_FF, D)), _resident((1, D)), _resident((QKV_WIDTH, D)),
                  _resident((D, POOL_WIDTH + 2 * D)), _resident((HEAD_DIM, tm)),
                  _resident((HEAD_DIM, tm)), table, table],
        out_specs=[tile(D), tile_t(N_HEADS * KV_WIDTH), tile(K_AUG), tile_t(N_KV_HEADS * V_ROWS),
                   tile(POOL_WIDTH), tile(2 * D), tile_t(KSQ_ROWS)],
        out_shape=[jax.ShapeDtypeStruct((B, T, D), F32),
                   jax.ShapeDtypeStruct((B, N_HEADS * KV_WIDTH, T), BF16),
                   jax.ShapeDtypeStruct((B, T, K_AUG), BF16),
                   jax.ShapeDtypeStruct((B, N_KV_HEADS * V_ROWS, T), BF16),
                   jax.ShapeDtypeStruct((B, T, POOL_WIDTH), F32),
                   jax.ShapeDtypeStruct((B, T, 2 * D), BF16),
                   jax.ShapeDtypeStruct((B, KSQ_ROWS, T), F32)],
        scratch_shapes=[pltpu.VMEM((tm, D_FF), BF16)],
        compiler_params=pltpu.CompilerParams(
            dimension_semantics=("parallel", "parallel"), vmem_limit_bytes=VMEM_LIMIT_BYTES),
        name="ffn1_proj",
    )(x, lw["n1"], lw["wg1"], lw["wu1"], lw["wd1"], lw["nm"], lw["wqkvt"], lw["wzg"],
      gq, gk, cos, sin)


def _attention(qt, k, vt, ksq):
    B, T, _ = k.shape
    tq = _tile(T, ATTN_Q_TILES)
    tk = _tile(T, ATTN_K_TILES)
    kmax = (jnp.sqrt(jnp.max(ksq[:, :N_KV_HEADS, :], axis=-1)) * 1.001).reshape(B * N_KV_HEADS)
    return pl.pallas_call(
        _attn_kernel,
        grid_spec=pltpu.PrefetchScalarGridSpec(
            num_scalar_prefetch=1,
            grid=(B, T // tq, T // tk),
            in_specs=[pl.BlockSpec((1, N_HEADS * KV_WIDTH, tq), lambda b, qi, ki, kmax: (b, 0, qi)),
                      pl.BlockSpec((1, tk, K_AUG), lambda b, qi, ki, kmax: (b, ki, 0)),
                      pl.BlockSpec((1, N_KV_HEADS * V_ROWS, tk), lambda b, qi, ki, kmax: (b, 0, ki))],
            out_specs=pl.BlockSpec((1, tq, ATTN_WIDTH), lambda b, qi, ki, kmax: (b, qi, 0)),
            scratch_shapes=[pltpu.VMEM((N_HEADS, tq), F32), pltpu.VMEM((N_HEADS * V_ROWS, tq), F32),
                            pltpu.VMEM((N_HEADS, K_AUG, tq), BF16), pltpu.SMEM((1,), jnp.int32)]),
        out_shape=jax.ShapeDtypeStruct((B, T, ATTN_WIDTH), BF16),
        compiler_params=pltpu.CompilerParams(
            dimension_semantics=("parallel", "parallel", "arbitrary"),
            vmem_limit_bytes=VMEM_LIMIT_BYTES),
        name="attention",
    )(kmax, qt, k, vt)


def _mix_ffn2_ple(h, o, z, sg, p, lw):
    B, T, D = h.shape
    tm = _tile(T)
    nt = T // tm
    halo_blocks = tm // POOL_HALO
    tile = lambda w: pl.BlockSpec((1, tm, w), lambda b, i: (b, i, 0))
    prev = pl.BlockSpec((1, POOL_HALO, POOL_WIDTH),
                        lambda b, i: (b, jnp.maximum(i * halo_blocks - 1, 0), 0))
    nxt = pl.BlockSpec((1, POOL_HALO, POOL_WIDTH),
                       lambda b, i: (b, jnp.minimum((i + 1) * halo_blocks, T // POOL_HALO - 1), 0))
    return pl.pallas_call(
        functools.partial(_mix_ffn2_ple_kernel, seq_len=T),
        grid=(B, nt),
        in_specs=[tile(D), tile(ATTN_WIDTH), tile(POOL_WIDTH), prev, nxt, tile(2 * D), tile(PLE_DIM),
                  _resident((ATTN_WIDTH, D)), _resident((POOL_WIDTH, POOL_WIDTH)),
                  _resident((1, POOL_WIDTH)), _resident((POOL_WIDTH, D)), _resident((D, D)),
                  _resident((1, D)), _resident((D, D_FF)), _resident((D, D_FF)), _resident((D_FF, D)),
                  _resident((1, D)), _resident((D, D)), _resident((PLE_DIM, D)), _resident((1, D))],
        out_specs=tile(D),
        out_shape=jax.ShapeDtypeStruct((B, T, D), F32),
        scratch_shapes=[pltpu.VMEM((tm + 2 * POOL_HALO, POOL_WIDTH), F32),
                        pltpu.VMEM((tm, D_FF), BF16)],
        compiler_params=pltpu.CompilerParams(
            dimension_semantics=("parallel", "parallel"), vmem_limit_bytes=VMEM_LIMIT_BYTES),
        name="mix_ffn2_ple",
    )(h, o, z, z, z, sg, p,
      lw["wua"], lw["wpool"], lw["pscale"], lw["wup"], lw["wout"],
      lw["n2"], lw["wg2"], lw["wu2"], lw["wd2"],
      lw["npg"], lw["wpg"], lw["wpp"], lw["npp"])


def _encoder_layer(x, p, lw):
    cos, sin = _rope_tables(x.shape[1])
    h, qt, k, vt, z, sg, ksq = _ffn1_proj(x, lw, cos, sin)
    o = _attention(qt, k, vt, ksq)
    return _mix_ffn2_ple(h, o, z, sg, p, lw)


def kernel(x_prompt, x_sample, p_prompt, p_sample, ffn1_norm, ffn1_w_gate, ffn1_w_up, ffn1_w_down,
           mix_norm, w_in, q_norm, k_norm, w_up_attn, pool_w, pool_scale, w_up_pool, w_out,
           ffn2_norm, ffn2_w_gate, ffn2_w_up, ffn2_w_down, ple_gate_norm, w_ple_gate, w_ple_proj,
           ple_post_norm):
    weights = (ffn1_norm, ffn1_w_gate, ffn1_w_up, ffn1_w_down, mix_norm, w_in, q_norm, k_norm,
               w_up_attn, pool_w, pool_scale, w_up_pool, w_out, ffn2_norm, ffn2_w_gate, ffn2_w_up,
               ffn2_w_down, ple_gate_norm, w_ple_gate, w_ple_proj, ple_post_norm)
    depth = ffn1_norm.shape[0]
    layers = [_prep_layer([w[i] for w in weights]) for i in range(depth)]

    def trunk(x, p):
        h = x
        for i in range(depth):
            h = _encoder_layer(h, p[i], layers[i])
        return h

    return (trunk(x_prompt, p_prompt), trunk(x_sample, p_sample))
```

```python
import functools

import jax
import jax.numpy as jnp
from jax import lax
from jax.experimental import pallas as pl
from jax.experimental.pallas import tpu as pltpu

F32 = jnp.float32
BF16 = jnp.bfloat16

D_MODEL = 1024
GRID_W = 64
N_HEADS = 8
N_KV_HEADS = 2
GROUP = N_HEADS // N_KV_HEADS
HEAD_DIM = 64
ATTN_WIDTH = N_HEADS * HEAD_DIM
KV_WIDTH = N_KV_HEADS * HEAD_DIM
QKV_WIDTH = ATTN_WIDTH + 2 * KV_WIDTH
POOL_WINDOWS = (2, 4, 8, 16)
POOL_WIDTH = D_MODEL // 2
POOL_GROUP_WIDTH = POOL_WIDTH // len(POOL_WINDOWS)
POOL_HALO = 8
assert all(w & (w - 1) == 0 and w // 2 <= POOL_HALO for w in POOL_WINDOWS)
D_FF = 2816
FF_CHUNK = 256
N_SUB = 2
PLE_DIM = 256
ROPE_THETA = 10000.0
EPS = 1e-6
SM_SCALE = HEAD_DIM ** -0.5
LOG2E = 1.4426950408889634
Q_SCALE = SM_SCALE * LOG2E
ONES_ROWS = 16
V_ROWS = HEAD_DIM + ONES_ROWS
K_AUG = 256
REF_ROWS = 16
KSQ_ROWS = 8
ATTN_FIXED_REF_MARGIN = 60.0
PROBE_KEYS = 128
ATTN_Q_TILES = (1024, 512, 256, 128)
ATTN_K_TILES = (2048, 1024, 512, 256, 128)
ONLINE_STRIP, ONLINE_LOOKAHEAD = 512, 2
FIXED_STRIP, FIXED_LOOKAHEAD = 256, 2

V7X_LANES = 128
VMEM_LIMIT_BYTES = 60 * 1024 * 1024

NT_DIMS = (((1,), (1,)), ((), ()))


def _dot(a, b):
    return jnp.dot(a, b, preferred_element_type=F32)


def _sigmoid(x):
    return 1.0 / (1.0 + jnp.exp(-x))


def _rmsnorm(x, g):
    ms = jnp.mean(x * x, axis=-1, keepdims=True)
    return x * lax.rsqrt(ms + EPS) * g


def _row_slices(tm):
    sub = tm // N_SUB
    return [slice(s * sub, (s + 1) * sub) for s in range(N_SUB)]


def _swiglu(xns, rows, wg_ref, wu_ref, wd_ref, a_scr):
    for c0 in range(0, D_FF, FF_CHUNK):
        cols = slice(c0, min(c0 + FF_CHUNK, D_FF))
        for xn, r in zip(xns, rows):
            g = _dot(xn, wg_ref[:, cols])
            u = _dot(xn, wu_ref[:, cols])
            a_scr[r, cols] = (g * _sigmoid(g) * u).astype(BF16)
    return [_dot(a_scr[r, :], wd_ref[...]) for r in rows]


def _head_norm_rope_t(blk, gain, cos, sin):
    ms = jnp.mean(blk * blk, axis=0, keepdims=True)
    y = blk * lax.rsqrt(ms + EPS) * gain
    q4 = HEAD_DIM // 4
    partner = jnp.concatenate([y[q4:2 * q4], y[:q4], y[3 * q4:], y[2 * q4:3 * q4]], axis=0)
    return y * cos + partner * sin


def _ffn1_proj_kernel(x_ref, n1_ref, wg_ref, wu_ref, wd_ref, nm_ref, wqkvt_ref, wzg_ref,
                      gq_ref, gk_ref, cos_ref, sin_ref,
                      h_ref, qt_ref, k_ref, vt_ref, z_ref, sg_ref, ksq_ref, a_scr):
    rows = _row_slices(x_ref.shape[1])
    xs = [x_ref[0, r, :] for r in rows]
    xns = [_rmsnorm(x, n1_ref[...]).astype(BF16) for x in xs]
    ys = _swiglu(xns, rows, wg_ref, wu_ref, wd_ref, a_scr)
    hs = [x + 0.5 * y for x, y in zip(xs, ys)]
    for r, h in zip(rows, hs):
        h_ref[0, r, :] = h

    uns = [_rmsnorm(h, nm_ref[...]).astype(BF16) for h in hs]
    qkvts = [lax.dot_general(wqkvt_ref[...], un, NT_DIMS, preferred_element_type=F32) for un in uns]
    zgs = [_dot(un, wzg_ref[...]) for un in uns]
    for r, qkvt, zg in zip(rows, qkvts, zgs):
        cos = cos_ref[:, r]
        sin = sin_ref[:, r]
        gq = gq_ref[:, r]
        gk = gk_ref[:, r]
        for hd in range(N_HEADS):
            blk = qkvt[hd * HEAD_DIM:(hd + 1) * HEAD_DIM]
            qh = (_head_norm_rope_t(blk, gq, cos, sin) * Q_SCALE).astype(BF16)
            slabs = [jnp.zeros_like(qh)] * N_KV_HEADS
            slabs[hd // GROUP] = qh
            qt_ref[0, hd * KV_WIDTH:(hd + 1) * KV_WIDTH, r] = jnp.concatenate(slabs, axis=0)
        kt = jnp.concatenate(
            [_head_norm_rope_t(qkvt[ATTN_WIDTH + g * HEAD_DIM:ATTN_WIDTH + (g + 1) * HEAD_DIM], gk, cos, sin)
             for g in range(N_KV_HEADS)], axis=0)
        ktb = kt.astype(BF16)
        sub = ktb.shape[1]
        ones_col = (lax.broadcasted_iota(jnp.int32, (sub, K_AUG - KV_WIDTH), 1) == 0).astype(BF16)
        k_ref[0, r, :] = jnp.concatenate([kt.T.astype(BF16), ones_col], axis=1)
        kf = ktb.astype(F32)
        ksq = [jnp.sum(kf[g * HEAD_DIM:(g + 1) * HEAD_DIM] ** 2, axis=0, keepdims=True) for g in range(N_KV_HEADS)]
        ksq_ref[0, :, r] = jnp.concatenate(ksq + [jnp.zeros((KSQ_ROWS - N_KV_HEADS, sub), F32)], axis=0)
        ones = jnp.ones((ONES_ROWS, qkvt.shape[1]), BF16)
        for g in range(N_KV_HEADS):
            v0 = ATTN_WIDTH + KV_WIDTH + g * HEAD_DIM
            vt_ref[0, g * V_ROWS:(g + 1) * V_ROWS, r] = jnp.concatenate(
                [qkvt[v0:v0 + HEAD_DIM].astype(BF16), ones], axis=0)
        z_ref[0, r, :] = zg[:, :POOL_WIDTH]
        sg_ref[0, r, :] = _sigmoid(zg[:, POOL_WIDTH:]).astype(BF16)


def _run_units(tq, strip, lookahead, scores, accumulate):
    strip = min(strip, tq)
    units = [(hd, slice(s * strip, (s + 1) * strip)) for hd in range(N_HEADS) for s in range(tq // strip)]
    pending = [scores(*u) for u in units[:lookahead]]
    for i, unit in enumerate(units):
        if i + lookahead < len(units):
            pending.append(scores(*units[i + lookahead]))
        accumulate(*unit, pending.pop(0))


def _attn_kernel(kmax_ref, qt_ref, k_ref, vt_ref, o_ref, m_scr, acc_scr, qa_scr, fixed_scr):
    b = pl.program_id(0)
    ki = pl.program_id(2)
    tq = qt_ref.shape[2]
    vt = vt_ref[0]

    def acc_rows(hd):
        return slice(hd * V_ROWS, (hd + 1) * V_ROWS)

    def v_aug(hd):
        g = hd // GROUP
        return vt[g * V_ROWS:(g + 1) * V_ROWS]

    @pl.when(ki == 0)
    def _start():
        acc_scr[...] = jnp.zeros(acc_scr.shape, F32)
        k_probe = k_ref[0, :PROBE_KEYS, :KV_WIDTH]
        first_row = lax.broadcasted_iota(jnp.int32, (REF_ROWS, tq), 0) == 0
        safe = None
        for hd in range(N_HEADS):
            slab = qt_ref[0, hd * KV_WIDTH:(hd + 1) * KV_WIDTH, :]
            m = jnp.max(_dot(k_probe, slab), axis=0, keepdims=True).astype(BF16).astype(F32)
            m_scr[hd:hd + 1, :] = m
            qa_scr[hd, :KV_WIDTH, :] = slab
            qa_scr[hd, KV_WIDTH:KV_WIDTH + REF_ROWS, :] = jnp.where(first_row, -m, 0.0).astype(BF16)
            qa_scr[hd, KV_WIDTH + REF_ROWS:, :] = jnp.zeros((K_AUG - KV_WIDTH - REF_ROWS, tq), BF16)
            sf = slab.astype(F32)
            q_norm = jnp.sqrt(jnp.sum(sf * sf, axis=0, keepdims=True))
            ok = q_norm * kmax_ref[b * N_KV_HEADS + hd // GROUP] <= m + ATTN_FIXED_REF_MARGIN
            safe = ok if safe is None else jnp.logical_and(safe, ok)
        fixed_scr[0] = (jnp.min(safe.astype(F32)) > 0.5).astype(jnp.int32)

    fixed = fixed_scr[0] == 1

    @pl.when(jnp.logical_not(fixed))
    def _online():
        k = k_ref[0, :, :KV_WIDTH]

        def scores(hd, cols):
            return _dot(k, qt_ref[0, hd * KV_WIDTH:(hd + 1) * KV_WIDTH, cols])

        def accumulate(hd, cols, st):
            rows = acc_rows(hd)
            m_prev = m_scr[hd:hd + 1, cols]
            m_new = jnp.maximum(m_prev, jnp.max(st, axis=0, keepdims=True)).astype(BF16).astype(F32)
            alpha = jnp.exp2(m_prev - m_new)
            pt = jnp.exp2(st - m_new).astype(BF16)
            acc_scr[rows, cols] = alpha * acc_scr[rows, cols] + _dot(v_aug(hd), pt)
            m_scr[hd:hd + 1, cols] = m_new

        _run_units(tq, ONLINE_STRIP, ONLINE_LOOKAHEAD, scores, accumulate)

    @pl.when(fixed)
    def _fixed_reference():
        k = k_ref[0]

        def scores(hd, cols):
            return _dot(k, qa_scr[hd, :, cols])

        def accumulate(hd, cols, st):
            rows = acc_rows(hd)
            acc_scr[rows, cols] = acc_scr[rows, cols] + _dot(v_aug(hd), jnp.exp2(st).astype(BF16))

        _run_units(tq, FIXED_STRIP, FIXED_LOOKAHEAD, scores, accumulate)

    @pl.when(ki == pl.num_programs(2) - 1)
    def _():
        ot = jnp.concatenate(
            [acc_scr[hd * V_ROWS:hd * V_ROWS + HEAD_DIM, :] / acc_scr[hd * V_ROWS + HEAD_DIM:hd * V_ROWS + HEAD_DIM + 1, :]
             for hd in range(N_HEADS)], axis=0)
        o_ref[0] = ot.T.astype(BF16)


def _pooled(zext_scr, r0, n, t0, seq_len):
    t = t0 + lax.broadcasted_iota(jnp.int32, (n, POOL_GROUP_WIDTH), 0)
    ext = n + 2 * POOL_HALO

    def ahead(v, d):
        return pltpu.roll(v, ext - d, 0)

    parts = []
    for g, w in enumerate(POOL_WINDOWS):
        left = w // 2
        right = w - 1 - left
        cols = slice(g * POOL_GROUP_WIDTH, (g + 1) * POOL_GROUP_WIDTH)
        zx = zext_scr[r0:r0 + ext, cols]
        fwd = zx
        span = 1
        while span < left:
            fwd = fwd + ahead(fwd, span)
            span *= 2
        total = (ahead(fwd, POOL_HALO - left) + ahead(fwd, POOL_HALO))[:n] if left < POOL_HALO \
            else (fwd + ahead(fwd, POOL_HALO))[:n]

        def clipped_mean(tot, pos, left=left, right=right):
            cnt = jnp.minimum(pos + right, seq_len - 1) - jnp.maximum(pos - left, 0) + 1
            return tot / cnt.astype(F32)

        mean = jnp.concatenate([clipped_mean(total[:POOL_HALO], t[:POOL_HALO]),
                                total[POOL_HALO:n - POOL_HALO] * (1.0 / w),
                                clipped_mean(total[n - POOL_HALO:], t[n - POOL_HALO:])], axis=0)
        parts.append(mean - zx[POOL_HALO:POOL_HALO + n])
    return jnp.concatenate(parts, axis=1)


def _mix_ffn2_ple_kernel(h_ref, o_ref, z_ref, zprev_ref, znext_ref, sg_ref, p_ref,
                         wua_ref, wpool_ref, pscale_ref, wup_ref, wout_ref,
                         n2_ref, wg_ref, wu_ref, wd_ref,
                         npg_ref, wpg_ref, wpp_ref, npp_ref,
                         y_ref, zext_scr, a_scr, *, seq_len):
    i = pl.program_id(1)
    tm = h_ref.shape[1]
    rows = _row_slices(tm)

    zext_scr[:POOL_HALO, :] = jnp.where(i > 0, zprev_ref[0], 0.0)
    zext_scr[POOL_HALO:POOL_HALO + tm, :] = z_ref[0]
    zext_scr[POOL_HALO + tm:, :] = jnp.where(i < pl.num_programs(1) - 1, znext_ref[0], 0.0)

    avals = [_dot(o_ref[0, r, :], wua_ref[...]) for r in rows]
    eraw = [_dot(p_ref[0, r, :].astype(BF16), wpp_ref[...]) for r in rows]
    hs = []
    for r, a in zip(rows, avals):
        pooled = _pooled(zext_scr, r.start, r.stop - r.start, i * tm + r.start, seq_len).astype(BF16)
        mixed = _dot(pooled, wpool_ref[...]) * pscale_ref[...]
        b = _dot(mixed.astype(BF16), wup_ref[...])
        sg = sg_ref[0, r, :]
        m = sg[:, :D_MODEL].astype(F32) * a + sg[:, D_MODEL:].astype(F32) * b
        hs.append(h_ref[0, r, :] + _dot(m.astype(BF16), wout_ref[...]))

    xns = [_rmsnorm(h, n2_ref[...]).astype(BF16) for h in hs]
    ys = _swiglu(xns, rows, wg_ref, wu_ref, wd_ref, a_scr)
    hs = [h + 0.5 * y for h, y in zip(hs, ys)]

    es = [_rmsnorm(e, npp_ref[...]) for e in eraw]
    for r, h, e in zip(rows, hs, es):
        gate = _sigmoid(_dot(_rmsnorm(h, npg_ref[...]).astype(BF16), wpg_ref[...]))
        y_ref[0, r, :] = h + gate * e


def _tile(n, candidates=(512, 256, 128)):
    for c in candidates:
        if n % c == 0:
            return c
    raise ValueError(f"sequence length {n} must be a multiple of {candidates[-1]}")


def _resident(shape):
    return pl.BlockSpec(shape, lambda *_: (0,) * len(shape), pipeline_mode=pl.Buffered(1))


def _rope_tables(seq_len):
    rows = seq_len // GRID_W
    row = jnp.repeat(jnp.arange(rows, dtype=F32), GRID_W)
    col = jnp.tile(jnp.arange(GRID_W, dtype=F32), rows)
    half = HEAD_DIM // 2
    inv = ROPE_THETA ** (-jnp.arange(0, half, 2, dtype=F32) / half)
    ang_r = (row[:, None] * inv).T
    ang_c = (col[:, None] * inv).T
    cos = jnp.concatenate([jnp.cos(ang_r)] * 2 + [jnp.cos(ang_c)] * 2, axis=0)
    sin = jnp.concatenate([-jnp.sin(ang_r), jnp.sin(ang_r), -jnp.sin(ang_c), jnp.sin(ang_c)], axis=0)
    return cos, sin


def _prep_layer(w):
    (ffn1_norm, ffn1_w_gate, ffn1_w_up, ffn1_w_down, mix_norm, w_in, q_norm, k_norm,
     w_up_attn, pool_w, pool_scale, w_up_pool, w_out, ffn2_norm, ffn2_w_gate, ffn2_w_up,
     ffn2_w_down, ple_gate_norm, w_ple_gate, w_ple_proj, ple_post_norm) = w
    row = lambda v: v.reshape(1, -1).astype(F32)
    pool_bd = jnp.zeros((POOL_WIDTH, POOL_WIDTH), F32)
    for g in range(len(POOL_WINDOWS)):
        s = slice(g * POOL_GROUP_WIDTH, (g + 1) * POOL_GROUP_WIDTH)
        pool_bd = pool_bd.at[s, s].set(pool_w[g])
    return dict(
        n1=row(ffn1_norm), wg1=ffn1_w_gate.astype(BF16), wu1=ffn1_w_up.astype(BF16),
        wd1=ffn1_w_down.astype(BF16), nm=row(mix_norm),
        wqkvt=w_in[:, :QKV_WIDTH].T.astype(BF16), wzg=w_in[:, QKV_WIDTH:].astype(BF16),
        q_norm=q_norm.astype(F32), k_norm=k_norm.astype(F32),
        wua=w_up_attn.astype(BF16), wpool=pool_bd.astype(BF16), pscale=row(pool_scale),
        wup=w_up_pool.astype(BF16), wout=w_out.astype(BF16),
        n2=row(ffn2_norm), wg2=ffn2_w_gate.astype(BF16), wu2=ffn2_w_up.astype(BF16),
        wd2=ffn2_w_down.astype(BF16),
        npg=row(ple_gate_norm), wpg=w_ple_gate.astype(BF16), wpp=w_ple_proj.astype(BF16),
        npp=row(ple_post_norm))


def _ffn1_proj(x, lw, cos, sin):
    B, T, D = x.shape
    tm = _tile(T)
    nt = T // tm
    gq = jnp.broadcast_to(lw["q_norm"][:, None], (HEAD_DIM, tm))
    gk = jnp.broadcast_to(lw["k_norm"][:, None], (HEAD_DIM, tm))
    tile = lambda w: pl.BlockSpec((1, tm, w), lambda b, i: (b, i, 0))
    tile_t = lambda r: pl.BlockSpec((1, r, tm), lambda b, i: (b, 0, i))
    table = pl.BlockSpec((HEAD_DIM, tm), lambda b, i: (0, i))
    return pl.pallas_call(
        _ffn1_proj_kernel,
        grid=(B, nt),
        in_specs=[tile(D), _resident((1, D)), _resident((D, D_FF)), _resident((D, D_FF)),
                  _resident((D_FF, D)), _resident((1, D)), _resident((QKV_WIDTH, D)),
                  _resident((D, POOL_WIDTH + 2 * D)), _resident((HEAD_DIM, tm)),
                  _resident((HEAD_DIM, tm)), table, table],
        out_specs=[tile(D), tile_t(N_HEADS * KV_WIDTH), tile(K_AUG), tile_t(N_KV_HEADS * V_ROWS),
                   tile(POOL_WIDTH), tile(2 * D), tile_t(KSQ_ROWS)],
        out_shape=[jax.ShapeDtypeStruct((B, T, D), F32),
                   jax.ShapeDtypeStruct((B, N_HEADS * KV_WIDTH, T), BF16),
                   jax.ShapeDtypeStruct((B, T, K_AUG), BF16),
                   jax.ShapeDtypeStruct((B, N_KV_HEADS * V_ROWS, T), BF16),
                   jax.ShapeDtypeStruct((B, T, POOL_WIDTH), F32),
                   jax.ShapeDtypeStruct((B, T, 2 * D), BF16),
                   jax.ShapeDtypeStruct((B, KSQ_ROWS, T), F32)],
        scratch_shapes=[pltpu.VMEM((tm, D_FF), BF16)],
        compiler_params=pltpu.CompilerParams(
            dimension_semantics=("parallel", "parallel"), vmem_limit_bytes=VMEM_LIMIT_BYTES),
        name="ffn1_proj",
    )(x, lw["n1"], lw["wg1"], lw["wu1"], lw["wd1"], lw["nm"], lw["wqkvt"], lw["wzg"],
      gq, gk, cos, sin)


def _attention(qt, k, vt, ksq):
    B, T, _ = k.shape
    tq = _tile(T, ATTN_Q_TILES)
    tk = _tile(T, ATTN_K_TILES)
    kmax = (jnp.sqrt(jnp.max(ksq[:, :N_KV_HEADS, :], axis=-1)) * 1.001).reshape(B * N_KV_HEADS)
    return pl.pallas_call(
        _attn_kernel,
        grid_spec=pltpu.PrefetchScalarGridSpec(
            num_scalar_prefetch=1,
            grid=(B, T // tq, T // tk),
            in_specs=[pl.BlockSpec((1, N_HEADS * KV_WIDTH, tq), lambda b, qi, ki, kmax: (b, 0, qi)),
                      pl.BlockSpec((1, tk, K_AUG), lambda b, qi, ki, kmax: (b, ki, 0)),
                      pl.BlockSpec((1, N_KV_HEADS * V_ROWS, tk), lambda b, qi, ki, kmax: (b, 0, ki))],
            out_specs=pl.BlockSpec((1, tq, ATTN_WIDTH), lambda b, qi, ki, kmax: (b, qi, 0)),
            scratch_shapes=[pltpu.VMEM((N_HEADS, tq), F32), pltpu.VMEM((N_HEADS * V_ROWS, tq), F32),
                            pltpu.VMEM((N_HEADS, K_AUG, tq), BF16), pltpu.SMEM((1,), jnp.int32)]),
        out_shape=jax.ShapeDtypeStruct((B, T, ATTN_WIDTH), BF16),
        compiler_params=pltpu.CompilerParams(
            dimension_semantics=("parallel", "parallel", "arbitrary"),
            vmem_limit_bytes=VMEM_LIMIT_BYTES),
        name="attention",
    )(kmax, qt, k, vt)


def _mix_ffn2_ple(h, o, z, sg, p, lw):
    B, T, D = h.shape
    tm = _tile(T)
    nt = T // tm
    halo_blocks = tm // POOL_HALO
    tile = lambda w: pl.BlockSpec((1, tm, w), lambda b, i: (b, i, 0))
    prev = pl.BlockSpec((1, POOL_HALO, POOL_WIDTH),
                        lambda b, i: (b, jnp.maximum(i * halo_blocks - 1, 0), 0))
    nxt = pl.BlockSpec((1, POOL_HALO, POOL_WIDTH),
                       lambda b, i: (b, jnp.minimum((i + 1) * halo_blocks, T // POOL_HALO - 1), 0))
    return pl.pallas_call(
        functools.partial(_mix_ffn2_ple_kernel, seq_len=T),
        grid=(B, nt),
        in_specs=[tile(D), tile(ATTN_WIDTH), tile(POOL_WIDTH), prev, nxt, tile(2 * D), tile(PLE_DIM),
                  _resident((ATTN_WIDTH, D)), _resident((POOL_WIDTH, POOL_WIDTH)),
                  _resident((1, POOL_WIDTH)), _resident((POOL_WIDTH, D)), _resident((D, D)),
                  _resident((1, D)), _resident((D, D_FF)), _resident((D, D_FF)), _resident((D_FF, D)),
                  _resident((1, D)), _resident((D, D)), _resident((PLE_DIM, D)), _resident((1, D))],
        out_specs=tile(D),
        out_shape=jax.ShapeDtypeStruct((B, T, D), F32),
        scratch_shapes=[pltpu.VMEM((tm + 2 * POOL_HALO, POOL_WIDTH), F32),
                        pltpu.VMEM((tm, D_FF), BF16)],
        compiler_params=pltpu.CompilerParams(
            dimension_semantics=("parallel", "parallel"), vmem_limit_bytes=VMEM_LIMIT_BYTES),
        name="mix_ffn2_ple",
    )(h, o, z, z, z, sg, p,
      lw["wua"], lw["wpool"], lw["pscale"], lw["wup"], lw["wout"],
      lw["n2"], lw["wg2"], lw["wu2"], lw["wd2"],
      lw["npg"], lw["wpg"], lw["wpp"], lw["npp"])


def _encoder_layer(x, p, lw):
    cos, sin = _rope_tables(x.shape[1])
    h, qt, k, vt, z, sg, ksq = _ffn1_proj(x, lw, cos, sin)
    o = _attention(qt, k, vt, ksq)
    return _mix_ffn2_ple(h, o, z, sg, p, lw)


def kernel(x_prompt, x_sample, p_prompt, p_sample, ffn1_norm, ffn1_w_gate, ffn1_w_up, ffn1_w_down,
           mix_norm, w_in, q_norm, k_norm, w_up_attn, pool_w, pool_scale, w_up_pool, w_out,
           ffn2_norm, ffn2_w_gate, ffn2_w_up, ffn2_w_down, ple_gate_norm, w_ple_gate, w_ple_proj,
           ple_post_norm):
    weights = (ffn1_norm, ffn1_w_gate, ffn1_w_up, ffn1_w_down, mix_norm, w_in, q_norm, k_norm,
               w_up_attn, pool_w, pool_scale, w_up_pool, w_out, ffn2_norm, ffn2_w_gate, ffn2_w_up,
               ffn2_w_down, ple_gate_norm, w_ple_gate, w_ple_proj, ple_post_norm)
    depth = ffn1_norm.shape[0]
    layers = [_prep_layer([w[i] for w in weights]) for i in range(depth)]

    def trunk(x, p):
        h = x
        for i in range(depth):
            h = _encoder_layer(h, p[i], layers[i])
        return h

    return (trunk(x_prompt, p_prompt), trunk(x_sample, p_sample))
```

```python
import functools

import jax
import jax.numpy as jnp
from jax import lax
from jax.experimental import pallas as pl
from jax.experimental.pallas import tpu as pltpu

F32 = jnp.float32
BF16 = jnp.bfloat16

D_MODEL = 1024
GRID_W = 64
N_HEADS = 8
N_KV_HEADS = 2
GROUP = N_HEADS // N_KV_HEADS
HEAD_DIM = 64
ATTN_WIDTH = N_HEADS * HEAD_DIM
KV_WIDTH = N_KV_HEADS * HEAD_DIM
QKV_WIDTH = ATTN_WIDTH + 2 * KV_WIDTH
POOL_WINDOWS = (2, 4, 8, 16)
POOL_WIDTH = D_MODEL // 2
POOL_GROUP_WIDTH = POOL_WIDTH // len(POOL_WINDOWS)
POOL_HALO = 8
assert all(w & (w - 1) == 0 and w // 2 <= POOL_HALO for w in POOL_WINDOWS)
D_FF = 2816
FF_CHUNK = 256
N_SUB = 2
PLE_DIM = 256
ROPE_THETA = 10000.0
EPS = 1e-6
SM_SCALE = HEAD_DIM ** -0.5
LOG2E = 1.4426950408889634
Q_SCALE = SM_SCALE * LOG2E
ONES_ROWS = 16
V_ROWS = HEAD_DIM + ONES_ROWS
K_AUG = 256
REF_ROWS = 16
KSQ_ROWS = 8
ATTN_FIXED_REF_MARGIN = 60.0
PROBE_KEYS = 128
ATTN_Q_TILES = (1024, 512, 256, 128)
ATTN_K_TILES = (2048, 1024, 512, 256, 128)
ONLINE_STRIP, ONLINE_LOOKAHEAD = 512, 2
FIXED_STRIP, FIXED_LOOKAHEAD = 256, 2

V7X_LANES = 128
VMEM_LIMIT_BYTES = 60 * 1024 * 1024

NT_DIMS = (((1,), (1,)), ((), ()))


def _dot(a, b):
    return jnp.dot(a, b, preferred_element_type=F32)


def _sigmoid(x):
    return 1.0 / (1.0 + jnp.exp(-x))


def _rmsnorm(x, g):
    ms = jnp.mean(x * x, axis=-1, keepdims=True)
    return x * lax.rsqrt(ms + EPS) * g


def _row_slices(tm):
    sub = tm // N_SUB
    return [slice(s * sub, (s + 1) * sub) for s in range(N_SUB)]


def _swiglu(xns, rows, wg_ref, wu_ref, wd_ref, a_scr):
    for c0 in range(0, D_FF, FF_CHUNK):
        cols = slice(c0, min(c0 + FF_CHUNK, D_FF))
        for xn, r in zip(xns, rows):
            g = _dot(xn, wg_ref[:, cols])
            u = _dot(xn, wu_ref[:, cols])
            a_scr[r, cols] = (g * _sigmoid(g) * u).astype(BF16)
    return [_dot(a_scr[r, :], wd_ref[...]) for r in rows]


def _head_norm_rope_t(blk, gain, cos, sin):
    ms = jnp.mean(blk * blk, axis=0, keepdims=True)
    y = blk * lax.rsqrt(ms + EPS) * gain
    q4 = HEAD_DIM // 4
    partner = jnp.concatenate([y[q4:2 * q4], y[:q4], y[3 * q4:], y[2 * q4:3 * q4]], axis=0)
    return y * cos + partner * sin


def _ffn1_proj_kernel(x_ref, n1_ref, wg_ref, wu_ref, wd_ref, nm_ref, wqkvt_ref, wzg_ref,
                      gq_ref, gk_ref, cos_ref, sin_ref,
                      h_ref, qt_ref, k_ref, vt_ref, z_ref, sg_ref, ksq_ref, a_scr):
    rows = _row_slices(x_ref.shape[1])
    xs = [x_ref[0, r, :] for r in rows]
    xns = [_rmsnorm(x, n1_ref[...]).astype(BF16) for x in xs]
    ys = _swiglu(xns, rows, wg_ref, wu_ref, wd_ref, a_scr)
    hs = [x + 0.5 * y for x, y in zip(xs, ys)]
    for r, h in zip(rows, hs):
        h_ref[0, r, :] = h

    uns = [_rmsnorm(h, nm_ref[...]).astype(BF16) for h in hs]
    qkvts = [lax.dot_general(wqkvt_ref[...], un, NT_DIMS, preferred_element_type=F32) for un in uns]
    zgs = [_dot(un, wzg_ref[...]) for un in uns]
    for r, qkvt, zg in zip(rows, qkvts, zgs):
        cos = cos_ref[:, r]
        sin = sin_ref[:, r]
        gq = gq_ref[:, r]
        gk = gk_ref[:, r]
        for hd in range(N_HEADS):
            blk = qkvt[hd * HEAD_DIM:(hd + 1) * HEAD_DIM]
            qh = (_head_norm_rope_t(blk, gq, cos, sin) * Q_SCALE).astype(BF16)
            slabs = [jnp.zeros_like(qh)] * N_KV_HEADS
            slabs[hd // GROUP] = qh
            qt_ref[0, hd * KV_WIDTH:(hd + 1) * KV_WIDTH, r] = jnp.concatenate(slabs, axis=0)
        kt = jnp.concatenate(
            [_head_norm_rope_t(qkvt[ATTN_WIDTH + g * HEAD_DIM:ATTN_WIDTH + (g + 1) * HEAD_DIM], gk, cos, sin)
             for g in range(N_KV_HEADS)], axis=0)
        ktb = kt.astype(BF16)
        sub = ktb.shape[1]
        ones_col = (lax.broadcasted_iota(jnp.int32, (sub, K_AUG - KV_WIDTH), 1) == 0).astype(BF16)
        k_ref[0, r, :] = jnp.concatenate([kt.T.astype(BF16), ones_col], axis=1)
        kf = ktb.astype(F32)
        ksq = [jnp.sum(kf[g * HEAD_DIM:(g + 1) * HEAD_DIM] ** 2, axis=0, keepdims=True) for g in range(N_KV_HEADS)]
        ksq_ref[0, :, r] = jnp.concatenate(ksq + [jnp.zeros((KSQ_ROWS - N_KV_HEADS, sub), F32)], axis=0)
        ones = jnp.ones((ONES_ROWS, qkvt.shape[1]), BF16)
        for g in range(N_KV_HEADS):
            v0 = ATTN_WIDTH + KV_WIDTH + g * HEAD_DIM
            vt_ref[0, g * V_ROWS:(g + 1) * V_ROWS, r] = jnp.concatenate(
                [qkvt[v0:v0 + HEAD_DIM].astype(BF16), ones], axis=0)
        z_ref[0, r, :] = zg[:, :POOL_WIDTH]
        sg_ref[0, r, :] = _sigmoid(zg[:, POOL_WIDTH:]).astype(BF16)


def _run_units(tq, strip, lookahead, scores, accumulate):
    strip = min(strip, tq)
    units = [(hd, slice(s * strip, (s + 1) * strip)) for hd in range(N_HEADS) for s in range(tq // strip)]
    pending = [scores(*u) for u in units[:lookahead]]
    for i, unit in enumerate(units):
        if i + lookahead < len(units):
            pending.append(scores(*units[i + lookahead]))
        accumulate(*unit, pending.pop(0))


def _attn_kernel(kmax_ref, qt_ref, k_ref, vt_ref, o_ref, m_scr, acc_scr, qa_scr, fixed_scr):
    b = pl.program_id(0)
    ki = pl.program_id(2)
    tq = qt_ref.shape[2]
    vt = vt_ref[0]

    def acc_rows(hd):
        return slice(hd * V_ROWS, (hd + 1) * V_ROWS)

    def v_aug(hd):
        g = hd // GROUP
        return vt[g * V_ROWS:(g + 1) * V_ROWS]

    @pl.when(ki == 0)
    def _start():
        acc_scr[...] = jnp.zeros(acc_scr.shape, F32)
        k_probe = k_ref[0, :PROBE_KEYS, :KV_WIDTH]
        first_row = lax.broadcasted_iota(jnp.int32, (REF_ROWS, tq), 0) == 0
        safe = None
        for hd in range(N_HEADS):
            slab = qt_ref[0, hd * KV_WIDTH:(hd + 1) * KV_WIDTH, :]
            m = jnp.max(_dot(k_probe, slab), axis=0, keepdims=True).astype(BF16).astype(F32)
            m_scr[hd:hd + 1, :] = m
            qa_scr[hd, :KV_WIDTH, :] = slab
            qa_scr[hd, KV_WIDTH:KV_WIDTH + REF_ROWS, :] = jnp.where(first_row, -m, 0.0).astype(BF16)
            qa_scr[hd, KV_WIDTH + REF_ROWS:, :] = jnp.zeros((K_AUG - KV_WIDTH - REF_ROWS, tq), BF16)
            sf = slab.astype(F32)
            q_norm = jnp.sqrt(jnp.sum(sf * sf, axis=0, keepdims=True))
            ok = q_norm * kmax_ref[b * N_KV_HEADS + hd // GROUP] <= m + ATTN_FIXED_REF_MARGIN
            safe = ok if safe is None else jnp.logical_and(safe, ok)
        fixed_scr[0] = (jnp.min(safe.astype(F32)) > 0.5).astype(jnp.int32)

    fixed = fixed_scr[0] == 1

    @pl.when(jnp.logical_not(fixed))
    def _online():
        k = k_ref[0, :, :KV_WIDTH]

        def scores(hd, cols):
            return _dot(k, qt_ref[0, hd * KV_WIDTH:(hd + 1) * KV_WIDTH, cols])

        def accumulate(hd, cols, st):
            rows = acc_rows(hd)
            m_prev = m_scr[hd:hd + 1, cols]
            m_new = jnp.maximum(m_prev, jnp.max(st, axis=0, keepdims=True)).astype(BF16).astype(F32)
            alpha = jnp.exp2(m_prev - m_new)
            pt = jnp.exp2(st - m_new).astype(BF16)
            acc_scr[rows, cols] = alpha * acc_scr[rows, cols] + _dot(v_aug(hd), pt)
            m_scr[hd:hd + 1, cols] = m_new

        _run_units(tq, ONLINE_STRIP, ONLINE_LOOKAHEAD, scores, accumulate)

    @pl.when(fixed)
    def _fixed_reference():
        k = k_ref[0]

        def scores(hd, cols):
            return _dot(k, qa_scr[hd, :, cols])

        def accumulate(hd, cols, st):
            rows = acc_rows(hd)
            acc_scr[rows, cols] = acc_scr[rows, cols] + _dot(v_aug(hd), jnp.exp2(st).astype(BF16))

        _run_units(tq, FIXED_STRIP, FIXED_LOOKAHEAD, scores, accumulate)

    @pl.when(ki == pl.num_programs(2) - 1)
    def _():
        ot = jnp.concatenate(
            [acc_scr[hd * V_ROWS:hd * V_ROWS + HEAD_DIM, :] * (1.0 / acc_scr[hd * V_ROWS + HEAD_DIM:hd * V_ROWS + HEAD_DIM + 1, :])
             for hd in range(N_HEADS)], axis=0)
        o_ref[0] = ot.T.astype(BF16)


def _pooled(zext_scr, r0, n, t0, seq_len):
    t = t0 + lax.broadcasted_iota(jnp.int32, (n, POOL_GROUP_WIDTH), 0)
    ext = n + 2 * POOL_HALO

    def ahead(v, d):
        return pltpu.roll(v, ext - d, 0)

    parts = []
    for g, w in enumerate(POOL_WINDOWS):
        left = w // 2
        right = w - 1 - left
        cols = slice(g * POOL_GROUP_WIDTH, (g + 1) * POOL_GROUP_WIDTH)
        zx = zext_scr[r0:r0 + ext, cols]
        fwd = zx
        span = 1
        while span < left:
            fwd = fwd + ahead(fwd, span)
            span *= 2
        total = (ahead(fwd, POOL_HALO - left) + ahead(fwd, POOL_HALO))[:n] if left < POOL_HALO \
            else (fwd + ahead(fwd, POOL_HALO))[:n]

        def clipped_mean(tot, pos, left=left, right=right):
            cnt = jnp.minimum(pos + right, seq_len - 1) - jnp.maximum(pos - left, 0) + 1
            return tot / cnt.astype(F32)

        mean = jnp.concatenate([clipped_mean(total[:POOL_HALO], t[:POOL_HALO]),
                                total[POOL_HALO:n - POOL_HALO] * (1.0 / w),
                                clipped_mean(total[n - POOL_HALO:], t[n - POOL_HALO:])], axis=0)
        parts.append(mean - zx[POOL_HALO:POOL_HALO + n])
    return jnp.concatenate(parts, axis=1)


def _mix_ffn2_ple_kernel(h_ref, o_ref, z_ref, zprev_ref, znext_ref, sg_ref, p_ref,
                         wua_ref, wpool_ref, pscale_ref, wup_ref, wout_ref,
                         n2_ref, wg_ref, wu_ref, wd_ref,
                         npg_ref, wpg_ref, wpp_ref, npp_ref,
                         y_ref, zext_scr, a_scr, *, seq_len):
    i = pl.program_id(1)
    tm = h_ref.shape[1]
    rows = _row_slices(tm)

    zext_scr[:POOL_HALO, :] = jnp.where(i > 0, zprev_ref[0], 0.0)
    zext_scr[POOL_HALO:POOL_HALO + tm, :] = z_ref[0]
    zext_scr[POOL_HALO + tm:, :] = jnp.where(i < pl.num_programs(1) - 1, znext_ref[0], 0.0)

    avals = [_dot(o_ref[0, r, :], wua_ref[...]) for r in rows]
    eraw = [_dot(p_ref[0, r, :].astype(BF16), wpp_ref[...]) for r in rows]
    hs = []
    for r, a in zip(rows, avals):
        pooled = _pooled(zext_scr, r.start, r.stop - r.start, i * tm + r.start, seq_len).astype(BF16)
        mixed = _dot(pooled, wpool_ref[...]) * pscale_ref[...]
        b = _dot(mixed.astype(BF16), wup_ref[...])
        sg = sg_ref[0, r, :]
        m = sg[:, :D_MODEL].astype(F32) * a + sg[:, D_MODEL:].astype(F32) * b
        hs.append(h_ref[0, r, :] + _dot(m.astype(BF16), wout_ref[...]))

    xns = [_rmsnorm(h, n2_ref[...]).astype(BF16) for h in hs]
    ys = _swiglu(xns, rows, wg_ref, wu_ref, wd_ref, a_scr)
    hs = [h + 0.5 * y for h, y in zip(hs, ys)]

    es = [_rmsnorm(e, npp_ref[...]) for e in eraw]
    for r, h, e in zip(rows, hs, es):
        gate = _sigmoid(_dot(_rmsnorm(h, npg_ref[...]).astype(BF16), wpg_ref[...]))
        y_ref[0, r, :] = h + gate * e


def _tile(n, candidates=(512, 256, 128)):
    for c in candidates:
        if n % c == 0:
            return c
    raise ValueError(f"sequence length {n} must be a multiple of {candidates[-1]}")


def _resident(shape):
    return pl.BlockSpec(shape, lambda *_: (0,) * len(shape), pipeline_mode=pl.Buffered(1))


def _rope_tables(seq_len):
    rows = seq_len // GRID_W
    row = jnp.repeat(jnp.arange(rows, dtype=F32), GRID_W)
    col = jnp.tile(jnp.arange(GRID_W, dtype=F32), rows)
    half = HEAD_DIM // 2
    inv = ROPE_THETA ** (-jnp.arange(0, half, 2, dtype=F32) / half)
    ang_r = (row[:, None] * inv).T
    ang_c = (col[:, None] * inv).T
    cos = jnp.concatenate([jnp.cos(ang_r)] * 2 + [jnp.cos(ang_c)] * 2, axis=0)
    sin = jnp.concatenate([-jnp.sin(ang_r), jnp.sin(ang_r), -jnp.sin(ang_c), jnp.sin(ang_c)], axis=0)
    return cos, sin


def _prep_layer(w):
    (ffn1_norm, ffn1_w_gate, ffn1_w_up, ffn1_w_down, mix_norm, w_in, q_norm, k_norm,
     w_up_attn, pool_w, pool_scale, w_up_pool, w_out, ffn2_norm, ffn2_w_gate, ffn2_w_up,
     ffn2_w_down, ple_gate_norm, w_ple_gate, w_ple_proj, ple_post_norm) = w
    row = lambda v: v.reshape(1, -1).astype(F32)
    pool_bd = jnp.zeros((POOL_WIDTH, POOL_WIDTH), F32)
    for g in range(len(POOL_WINDOWS)):
        s = slice(g * POOL_GROUP_WIDTH, (g + 1) * POOL_GROUP_WIDTH)
        pool_bd = pool_bd.at[s, s].set(pool_w[g])
    return dict(
        n1=row(ffn1_norm), wg1=ffn1_w_gate.astype(BF16), wu1=ffn1_w_up.astype(BF16),
        wd1=ffn1_w_down.astype(BF16), nm=row(mix_norm),
        wqkvt=w_in[:, :QKV_WIDTH].T.astype(BF16), wzg=w_in[:, QKV_WIDTH:].astype(BF16),
        q_norm=q_norm.astype(F32), k_norm=k_norm.astype(F32),
        wua=w_up_attn.astype(BF16), wpool=pool_bd.astype(BF16), pscale=row(pool_scale),
        wup=w_up_pool.astype(BF16), wout=w_out.astype(BF16),
        n2=row(ffn2_norm), wg2=ffn2_w_gate.astype(BF16), wu2=ffn2_w_up.astype(BF16),
        wd2=ffn2_w_down.astype(BF16),
        npg=row(ple_gate_norm), wpg=w_ple_gate.astype(BF16), wpp=w_ple_proj.astype(BF16),
        npp=row(ple_post_norm))


def _ffn1_proj(x, lw, cos, sin):
    B, T, D = x.shape
    tm = _tile(T)
    nt = T // tm
    gq = jnp.broadcast_to(lw["q_norm"][:, None], (HEAD_DIM, tm))
    gk = jnp.broadcast_to(lw["k_norm"][:, None], (HEAD_DIM, tm))
    tile = lambda w: pl.BlockSpec((1, tm, w), lambda b, i: (b, i, 0))
    tile_t = lambda r: pl.BlockSpec((1, r, tm), lambda b, i: (b, 0, i))
    table = pl.BlockSpec((HEAD_DIM, tm), lambda b, i: (0, i))
    return pl.pallas_call(
        _ffn1_proj_kernel,
        grid=(B, nt),
        in_specs=[tile(D), _resident((1, D)), _resident((D, D_FF)), _resident((D, D_FF)),
                  _resident((D_FF, D)), _resident((1, D)), _resident((QKV_WIDTH, D)),
                  _resident((D, POOL_WIDTH + 2 * D)), _resident((HEAD_DIM, tm)),
                  _resident((HEAD_DIM, tm)), table, table],
        out_specs=[tile(D), tile_t(N_HEADS * KV_WIDTH), tile(K_AUG), tile_t(N_KV_HEADS * V_ROWS),
                   tile(POOL_WIDTH), tile(2 * D), tile_t(KSQ_ROWS)],
        out_shape=[jax.ShapeDtypeStruct((B, T, D), F32),
                   jax.ShapeDtypeStruct((B, N_HEADS * KV_WIDTH, T), BF16),
                   jax.ShapeDtypeStruct((B, T, K_AUG), BF16),
                   jax.ShapeDtypeStruct((B, N_KV_HEADS * V_ROWS, T), BF16),
                   jax.ShapeDtypeStruct((B, T, POOL_WIDTH), F32),
                   jax.ShapeDtypeStruct((B, T, 2 * D), BF16),
                   jax.ShapeDtypeStruct((B, KSQ_ROWS, T), F32)],
        scratch_shapes=[pltpu.VMEM((tm, D_FF), BF16)],
        compiler_params=pltpu.CompilerParams(
            dimension_semantics=("parallel", "parallel"), vmem_limit_bytes=VMEM_LIMIT_BYTES),
        name="ffn1_proj",
    )(x, lw["n1"], lw["wg1"], lw["wu1"], lw["wd1"], lw["nm"], lw["wqkvt"], lw["wzg"],
      gq, gk, cos, sin)


def _attention(qt, k, vt, ksq):
    B, T, _ = k.shape
    tq = _tile(T, ATTN_Q_TILES)
    tk = _tile(T, ATTN_K_TILES)
    kmax = (jnp.sqrt(jnp.max(ksq[:, :N_KV_HEADS, :], axis=-1)) * 1.001).reshape(B * N_KV_HEADS)
    return pl.pallas_call(
        _attn_kernel,
        grid_spec=pltpu.PrefetchScalarGridSpec(
            num_scalar_prefetch=1,
            grid=(B, T // tq, T // tk),
            in_specs=[pl.BlockSpec((1, N_HEADS * KV_WIDTH, tq), lambda b, qi, ki, kmax: (b, 0, qi)),
                      pl.BlockSpec((1, tk, K_AUG), lambda b, qi, ki, kmax: (b, ki, 0)),
                      pl.BlockSpec((1, N_KV_HEADS * V_ROWS, tk), lambda b, qi, ki, kmax: (b, 0, ki))],
            out_specs=pl.BlockSpec((1, tq, ATTN_WIDTH), lambda b, qi, ki, kmax: (b, qi, 0)),
            scratch_shapes=[pltpu.VMEM((N_HEADS, tq), F32), pltpu.VMEM((N_HEADS * V_ROWS, tq), F32),
                            pltpu.VMEM((N_HEADS, K_AUG, tq), BF16), pltpu.SMEM((1,), jnp.int32)]),
        out_shape=jax.ShapeDtypeStruct((B, T, ATTN_WIDTH), BF16),
        compiler_params=pltpu.CompilerParams(
            dimension_semantics=("parallel", "parallel", "arbitrary"),
            vmem_limit_bytes=VMEM_LIMIT_BYTES),
        name="attention",
    )(kmax, qt, k, vt)


def _mix_ffn2_ple(h, o, z, sg, p, lw):
    B, T, D = h.shape
    tm = _tile(T)
    nt = T // tm
    halo_blocks = tm // POOL_HALO
    tile = lambda w: pl.BlockSpec((1, tm, w), lambda b, i: (b, i, 0))
    prev = pl.BlockSpec((1, POOL_HALO, POOL_WIDTH),
                        lambda b, i: (b, jnp.maximum(i * halo_blocks - 1, 0), 0))
    nxt = pl.BlockSpec((1, POOL_HALO, POOL_WIDTH),
                       lambda b, i: (b, jnp.minimum((i + 1) * halo_blocks, T // POOL_HALO - 1), 0))
    return pl.pallas_call(
        functools.partial(_mix_ffn2_ple_kernel, seq_len=T),
        grid=(B, nt),
        in_specs=[tile(D), tile(ATTN_WIDTH), tile(POOL_WIDTH), prev, nxt, tile(2 * D), tile(PLE_DIM),
                  _resident((ATTN_WIDTH, D)), _resident((POOL_WIDTH, POOL_WIDTH)),
                  _resident((1, POOL_WIDTH)), _resident((POOL_WIDTH, D)), _resident((D, D)),
                  _resident((1, D)), _resident((D, D_FF)), _resident((D, D_FF)), _resident((D_FF, D)),
                  _resident((1, D)), _resident((D, D)), _resident((PLE_DIM, D)), _resident((1, D))],
        out_specs=tile(D),
        out_shape=jax.ShapeDtypeStruct((B, T, D), F32),
        scratch_shapes=[pltpu.VMEM((tm + 2 * POOL_HALO, POOL_WIDTH), F32),
                        pltpu.VMEM((tm, D_FF), BF16)],
        compiler_params=pltpu.CompilerParams(
            dimension_semantics=("parallel", "parallel"), vmem_limit_bytes=VMEM_LIMIT_BYTES),
        name="mix_ffn2_ple",
    )(h, o, z, z, z, sg, p,
      lw["wua"], lw["wpool"], lw["pscale"], lw["wup"], lw["wout"],
      lw["n2"], lw["wg2"], lw["wu2"], lw["wd2"],
      lw["npg"], lw["wpg"], lw["wpp"], lw["npp"])


def _encoder_layer(x, p, lw):
    cos, sin = _rope_tables(x.shape[1])
    h, qt, k, vt, z, sg, ksq = _ffn1_proj(x, lw, cos, sin)
    o = _attention(qt, k, vt, ksq)
    return _mix_ffn2_ple(h, o, z, sg, p, lw)


def kernel(x_prompt, x_sample, p_prompt, p_sample, ffn1_norm, ffn1_w_gate, ffn1_w_up, ffn1_w_down,
           mix_norm, w_in, q_norm, k_norm, w_up_attn, pool_w, pool_scale, w_up_pool, w_out,
           ffn2_norm, ffn2_w_gate, ffn2_w_up, ffn2_w_down, ple_gate_norm, w_ple_gate, w_ple_proj,
           ple_post_norm):
    weights = (ffn1_norm, ffn1_w_gate, ffn1_w_up, ffn1_w_down, mix_norm, w_in, q_norm, k_norm,
               w_up_attn, pool_w, pool_scale, w_up_pool, w_out, ffn2_norm, ffn2_w_gate, ffn2_w_up,
               ffn2_w_down, ple_gate_norm, w_ple_gate, w_ple_proj, ple_post_norm)
    depth = ffn1_norm.shape[0]
    layers = [_prep_layer([w[i] for w in weights]) for i in range(depth)]

    def trunk(x, p):
        h = x
        for i in range(depth):
            h = _encoder_layer(h, p[i], layers[i])
        return h

    return (trunk(x_prompt, p_prompt), trunk(x_sample, p_sample))
```
